```python
import math
import jax, jax.numpy as jnp
from jax import lax
import numpy as np

D_MODEL = 1024
BATCH = 2
SEQ = 16384
DEPTH = 1
DEC_BATCH = 16
DEC_SEQ = 4096
PAST_LEN = 128

N_META = 16
GRID_W = 64
NA_HEADS = 8
NA_HEAD_DIM = 64
NA_WIDTH = NA_HEADS * NA_HEAD_DIM
NA_WIN_ROWS = 8
NA_WIN_COLS = 16
WA_Q_HEADS = 8
WA_KV_HEADS = 2
WA_HEAD_DIM = 64
WA_WIDTH = WA_Q_HEADS * WA_HEAD_DIM
WA_KV_WIDTH = WA_KV_HEADS * WA_HEAD_DIM
WINDOW = 128
BLOCK = 128
T5_BUCKETS = 32
T5_MAX_DIST = 128
RMS_EPS = 1e-6
NEG_INF = -1e30
IN_SIZES = (NA_WIDTH, NA_WIDTH, NA_WIDTH, NA_WIDTH,
            WA_WIDTH, WA_KV_WIDTH, WA_KV_WIDTH, WA_WIDTH,
            D_MODEL, D_MODEL)
IN_WIDTH = sum(IN_SIZES)

kernel_name = 'hybrid_na_window_gqa_encoder'


def rms_norm(x, g):
    x32 = x.astype(jnp.float32)
    y = x32 * lax.rsqrt(jnp.mean(x32 * x32, axis=-1, keepdims=True) + RMS_EPS)
    return (y * g.astype(jnp.float32)).astype(x.dtype)


def t5_bucket(rel):
    half = T5_BUCKETS // 2
    exact = half // 2
    ret = jnp.where(rel > 0, half, 0)
    n = jnp.abs(rel)
    nf = jnp.maximum(n, 1).astype(jnp.float32)
    large = exact + (jnp.log(nf / exact) / math.log(T5_MAX_DIST / exact)
                     * (half - exact)).astype(jnp.int32)
    large = jnp.minimum(large, half - 1)
    return ret + jnp.where(n < exact, n, large)


def neighbourhood_attention(q, k, v, rpb, n):
    B, _, H, hd = q.shape
    rows = n // GRID_W
    kr = min(NA_WIN_ROWS, rows)
    kc = NA_WIN_COLS
    scale = hd ** -0.5
    f32 = jnp.float32
    qm, km, vm = q[:, :N_META], k[:, :N_META], v[:, :N_META]
    qg = q[:, N_META:].reshape(B, rows, GRID_W, H, hd)
    kg = k[:, N_META:].reshape(B, rows, GRID_W, H, hd)
    vg = v[:, N_META:].reshape(B, rows, GRID_W, H, hd)
    cols = jnp.arange(GRID_W)
    col_start = jnp.clip(cols - kc // 2, 0, GRID_W - kc)
    col_idx = col_start[:, None] + jnp.arange(kc)[None, :]
    col_rel = col_idx - cols[:, None] + (NA_WIN_COLS - 1)
    rpb_cols = rpb.astype(f32)[:, :, col_rel]

    def one_row(i):
        rs = jnp.clip(i - kr // 2, 0, rows - kr)
        k_rows = lax.dynamic_slice_in_dim(kg, rs, kr, axis=1)
        v_rows = lax.dynamic_slice_in_dim(vg, rs, kr, axis=1)
        k_win = k_rows[:, :, col_idx]
        v_win = v_rows[:, :, col_idx]
        q_row = lax.dynamic_index_in_dim(qg, i, axis=1, keepdims=False)
        row_rel = rs + jnp.arange(kr) - i + (NA_WIN_ROWS - 1)
        bias = rpb_cols[:, row_rel].transpose(0, 2, 1, 3)
        s_win = jnp.einsum('bjhd,bajchd->bhjac', q_row, k_win).astype(f32) * scale + bias
        s_meta = jnp.einsum('bjhd,bmhd->bhjm', q_row, km).astype(f32) * scale
        logits = jnp.concatenate([s_win.reshape(B, H, GRID_W, kr * kc), s_meta], axis=-1)
        p = jax.nn.softmax(logits, axis=-1).astype(v.dtype)
        p_win = p[..., :kr * kc].reshape(B, H, GRID_W, kr, kc)
        p_meta = p[..., kr * kc:]
        return (jnp.einsum('bhjac,bajchd->bjhd', p_win, v_win)
                + jnp.einsum('bhjm,bmhd->bjhd', p_meta, vm))

    og = lax.map(one_row, jnp.arange(rows))
    og = jnp.moveaxis(og, 0, 1).reshape(B, n, H, hd)
    s_m = jnp.einsum('bqhd,bmhd->bhqm', qm, km).astype(f32) * scale
    p_m = jax.nn.softmax(s_m, axis=-1).astype(v.dtype)
    o_meta = jnp.einsum('bhqm,bmhd->bqhd', p_m, vm)
    return jnp.concatenate([o_meta, og], axis=1)


def window_attention(q, k, v, t5_bias, sink, n):
    B, _, HQ, hd = q.shape
    HKV = k.shape[2]
    G = HQ // HKV
    nb = n // BLOCK
    C = 3 * BLOCK
    scale = hd ** -0.5
    f32 = jnp.float32
    bias_tab = t5_bias.astype(f32)
    sink_g = sink.astype(f32).reshape(HKV, G)
    km, vm = k[:, :N_META], v[:, :N_META]

    qr = q[:, N_META:].reshape(B, nb, BLOCK, HKV, G, hd)

    def band(t):
        tp = jnp.pad(t[:, N_META:], ((0, 0), (BLOCK, BLOCK), (0, 0), (0, 0)))
        tp = tp.reshape(B, nb + 2, BLOCK, HKV, hd)
        return jnp.concatenate([tp[:, :-2], tp[:, 1:-1], tp[:, 2:]], axis=2)

    k_band, v_band = band(k), band(v)
    qq = jnp.arange(BLOCK)
    kk = jnp.arange(C)
    blk = jnp.arange(nb)
    rel = kk[None, :] - BLOCK - qq[:, None]
    key_idx = blk[:, None] * BLOCK + kk[None, :] - BLOCK
    visible = (jnp.abs(rel) <= WINDOW)[None] & ((key_idx >= 0) & (key_idx < n))[:, None, :]
    band_bias = bias_tab[t5_bucket(rel)].reshape(BLOCK, C, HKV, G).transpose(2, 3, 0, 1)
    q_pos = N_META + blk[:, None] * BLOCK + qq[None, :]
    meta_rel = jnp.arange(N_META)[None, None, :] - q_pos[:, :, None]
    meta_bias = bias_tab[t5_bucket(meta_rel)].reshape(nb, BLOCK, N_META, HKV, G).transpose(0, 3, 4, 1, 2)

    s_band = jnp.einsum('bnqkgd,bnckd->bnkgqc', qr, k_band).astype(f32) * scale + band_bias
    s_band = jnp.where(visible[:, None, None], s_band, NEG_INF)
    s_meta = jnp.einsum('bnqkgd,bmkd->bnkgqm', qr, km).astype(f32) * scale + meta_bias
    sink_col = jnp.broadcast_to(sink_g[None, None, :, :, None, None], (B, nb, HKV, G, BLOCK, 1))
    p = jax.nn.softmax(jnp.concatenate([s_band, s_meta, sink_col], axis=-1), axis=-1).astype(v.dtype)
    o_real = (jnp.einsum('bnkgqc,bnckd->bnqkgd', p[..., :C], v_band)
              + jnp.einsum('bnkgqm,bmkd->bnqkgd', p[..., C:C + N_META], vm))
    o_real = o_real.reshape(B, n, HQ, hd)

    nk = N_META + WINDOW
    qm = q[:, :N_META].reshape(B, N_META, HKV, G, hd)
    k_lead, v_lead = k[:, :nk], v[:, :nk]
    rel_m = jnp.arange(nk)[None, :] - jnp.arange(N_META)[:, None]
    bias_m = bias_tab[t5_bucket(rel_m)].reshape(N_META, nk, HKV, G).transpose(2, 3, 0, 1)
    s_m = jnp.einsum('bqkgd,bckd->bkgqc', qm, k_lead).astype(f32) * scale + bias_m
    s_m = jnp.where(jnp.abs(rel_m) <= WINDOW, s_m, NEG_INF)
    sink_m = jnp.broadcast_to(sink_g[None, :, :, None, None], (B, HKV, G, N_META, 1))
    p_m = jax.nn.softmax(jnp.concatenate([s_m, sink_m], axis=-1), axis=-1)[..., :nk].astype(v.dtype)
    o_meta = jnp.einsum('bkgqc,bckd->bqkgd', p_m, v_lead).reshape(B, N_META, HQ, hd)
    return jnp.concatenate([o_meta, o_real], axis=1)


def mixer_layer(h, norm_g, w_in, na_rpb, sink, w_proj_a, w_proj_b, w_out, t5_bias):
    B, L, _ = h.shape
    n = L - N_META
    u = rms_norm(h, norm_g)
    proj = u @ w_in
    cuts = [int(c) for c in np.cumsum(IN_SIZES)[:-1]]
    qa, ka, va, za, qb, kb, vb, zb, ga, gb = jnp.split(proj, cuts, axis=-1)
    o_a = neighbourhood_attention(qa.reshape(B, L, NA_HEADS, NA_HEAD_DIM),
                                  ka.reshape(B, L, NA_HEADS, NA_HEAD_DIM),
                                  va.reshape(B, L, NA_HEADS, NA_HEAD_DIM),
                                  na_rpb, n).reshape(B, L, NA_WIDTH)
    o_b = window_attention(qb.reshape(B, L, WA_Q_HEADS, WA_HEAD_DIM),
                           kb.reshape(B, L, WA_KV_HEADS, WA_HEAD_DIM),
                           vb.reshape(B, L, WA_KV_HEADS, WA_HEAD_DIM),
                           t5_bias, sink, n).reshape(B, L, WA_WIDTH)
    y_a = (o_a * jax.nn.silu(za)) @ w_proj_a
    y_b = (o_b * jax.nn.silu(zb)) @ w_proj_b
    merged = jax.nn.sigmoid(ga) * y_a + jax.nn.sigmoid(gb) * y_b
    return merged @ w_out


def encode(x, meta_tokens, norm_g, w_in, na_rpb, sink_logit, w_proj_a, w_proj_b, w_out, t5_bias, final_g):
    B = x.shape[0]
    meta = jnp.broadcast_to(meta_tokens.astype(x.dtype)[None], (B, N_META, x.shape[-1]))
    h = jnp.concatenate([meta, x], axis=1)
    for l in range(DEPTH):
        h = h + mixer_layer(h, norm_g[l], w_in[l], na_rpb[l], sink_logit[l],
                            w_proj_a[l], w_proj_b[l], w_out[l], t5_bias)
    return rms_norm(h[:, N_META:], final_g)


def setup_inputs(seed: int = 0) -> dict:
    key = jax.random.key(seed)
    ks = jax.random.split(key, 12)
    nrm = jax.random.normal
    return {
        'x_prompt': nrm(ks[0], (BATCH, SEQ, D_MODEL), jnp.float32),
        'x_sample': nrm(ks[1], (DEC_BATCH, DEC_SEQ, D_MODEL), jnp.float32),
        'meta_tokens': nrm(ks[2], (N_META, D_MODEL), jnp.float32),
        'norm_g': 1.0 + 0.05 * nrm(ks[3], (DEPTH, D_MODEL), jnp.float32),
        'w_in': nrm(ks[4], (DEPTH, D_MODEL, IN_WIDTH), jnp.float32) * D_MODEL ** -0.5,
        'na_rpb': 0.5 * nrm(ks[5], (DEPTH, NA_HEADS, 2 * NA_WIN_ROWS - 1, 2 * NA_WIN_COLS - 1), jnp.float32),
        'sink_logit': nrm(ks[6], (DEPTH, WA_Q_HEADS), jnp.float32),
        'w_proj_a': nrm(ks[7], (DEPTH, NA_WIDTH, D_MODEL), jnp.float32) * NA_WIDTH ** -0.5,
        'w_proj_b': nrm(ks[8], (DEPTH, WA_WIDTH, D_MODEL), jnp.float32) * WA_WIDTH ** -0.5,
        'w_out': nrm(ks[9], (DEPTH, D_MODEL, D_MODEL), jnp.float32) * D_MODEL ** -0.5,
        't5_bias': 0.5 * nrm(ks[10], (T5_BUCKETS, WA_Q_HEADS), jnp.float32),
        'final_g': 1.0 + 0.05 * nrm(ks[11], (D_MODEL,), jnp.float32),
    }


def reference(x_prompt, x_sample, meta_tokens, norm_g, w_in, na_rpb, sink_logit, w_proj_a, w_proj_b, w_out, t5_bias, final_g):
    y_prompt = encode(x_prompt, meta_tokens, norm_g, w_in, na_rpb, sink_logit,
                      w_proj_a, w_proj_b, w_out, t5_bias, final_g)
    y_sample = encode(x_sample, meta_tokens, norm_g, w_in, na_rpb, sink_logit,
                      w_proj_a, w_proj_b, w_out, t5_bias, final_g)
    return (y_prompt, y_sample)
```

```python
import functools
import math

import jax
import jax.numpy as jnp
from jax import lax
from jax.experimental import pallas as pl
from jax.experimental.pallas import tpu as pltpu

F32 = jnp.float32
BF16 = jnp.bfloat16

D_MODEL = 1024
N_META = 16
GRID_W = 64
HEADS = 8
HEAD_DIM = 64
ATT_WIDTH = HEADS * HEAD_DIM
NA_WIN_ROWS = 8
NA_WIN_COLS = 16
WA_KV_HEADS = 2
WA_GROUP = HEADS // WA_KV_HEADS
WA_KV_WIDTH = WA_KV_HEADS * HEAD_DIM
WINDOW = 128
BLOCK = 128
T5_BUCKETS = 32
T5_MAX_DIST = 128
RMS_EPS = 1e-6
NEG_INF = -1e30
IN_WIDTH = 5376

COL_QA, COL_KA, COL_VA, COL_ZA, COL_QB, COL_ZB = 0, 512, 1024, 1536, 2048, 2560
COL_GA, COL_GB, COL_KB, COL_VB = 3072, 4096, 5120, 5248

NA_ROWS_PER_STEP = 4
NA_Q = NA_ROWS_PER_STEP * GRID_W
NA_KEY_SEGS = 3
PROJ_TM = 512
MIX_TM = 512
PROJ_N_CHUNK = 768
VMEM_LIMIT = 52 * 1024 * 1024

_NT = (((1,), (1,)), ((), ()))


def _t5_bucket(rel):
    half = T5_BUCKETS // 2
    exact = half // 2
    ret = jnp.where(rel > 0, half, 0)
    n = jnp.abs(rel)
    nf = jnp.maximum(n, 1).astype(F32)
    large = exact + (jnp.log(nf / exact) / math.log(T5_MAX_DIST / exact)
                     * (half - exact)).astype(jnp.int32)
    large = jnp.minimum(large, half - 1)
    return ret + jnp.where(n < exact, n, large)


def _norm_proj_body(x_ref, g_ref, w_ref, o_ref):
    x = x_ref[...]
    ms = jnp.mean(x * x, axis=-1, keepdims=True)
    u = ((x * lax.rsqrt(ms + RMS_EPS)) * g_ref[...]).astype(BF16)
    for j in range(0, IN_WIDTH, PROJ_N_CHUNK):
        o_ref[:, j:j + PROJ_N_CHUNK] = jnp.dot(
            u, w_ref[:, j:j + PROJ_N_CHUNK], preferred_element_type=F32).astype(BF16)


def _norm_proj(x2d, g, w, tm):
    m = x2d.shape[0]
    assert m % tm == 0
    return pl.pallas_call(
        _norm_proj_body,
        grid=(m // tm,),
        in_specs=[pl.BlockSpec((tm, D_MODEL), lambda i: (i, 0)),
                  pl.BlockSpec((1, D_MODEL), lambda i: (0, 0)),
                  pl.BlockSpec((D_MODEL, IN_WIDTH), lambda i: (0, 0))],
        out_specs=pl.BlockSpec((tm, IN_WIDTH), lambda i: (i, 0)),
        out_shape=jax.ShapeDtypeStruct((m, IN_WIDTH), BF16),
        compiler_params=pltpu.CompilerParams(
            dimension_semantics=("arbitrary",), vmem_limit_bytes=VMEM_LIMIT),
        name="norm_proj",
    )(x2d, g, w)


def _na_bias_tables(rpb):
    a = jnp.arange(NA_ROWS_PER_STEP)[:, None, None, None]
    j = jnp.arange(GRID_W)[None, :, None, None]
    t = jnp.arange(NA_KEY_SEGS * NA_ROWS_PER_STEP)[None, None, :, None]
    c = jnp.arange(GRID_W)[None, None, None, :]
    cs = jnp.clip(j - NA_WIN_COLS // 2, 0, GRID_W - NA_WIN_COLS)
    col_ok = (c >= cs) & (c < cs + NA_WIN_COLS)
    col_rel = jnp.clip(c - j + NA_WIN_COLS - 1, 0, 2 * NA_WIN_COLS - 2)
    row_rel = jnp.clip(t - a + NA_WIN_ROWS // 2 - 1, 0, 2 * NA_WIN_ROWS - 2)
    d = t - a
    row_ok = (
        (t >= NA_ROWS_PER_STEP) & (a >= 0),
        (d >= 0) & (d < NA_WIN_ROWS),
        (t < NA_WIN_ROWS) & (a >= 0),
    )
    vals = rpb.astype(F32)[:, row_rel, col_rel]
    tabs = [jnp.where((ok & col_ok)[None], vals, NEG_INF) for ok in row_ok]
    return jnp.stack(tabs).reshape(3, HEADS, NA_Q, NA_KEY_SEGS * NA_Q)


def _na_body(q_ref, kp_ref, kc_ref, kn_ref, vp_ref, vc_ref, vn_ref, km_ref, vm_ref,
             bias_ref, o_ref):
    k_refs = (kp_ref, kc_ref, kn_ref)
    v_refs = (vp_ref, vc_ref, vn_ref)
    for h in range(HEADS):
        sl = slice(h * HEAD_DIM, (h + 1) * HEAD_DIM)
        q = q_ref[0, :, sl]
        s = [lax.dot_general(q, kr[0, :, sl], _NT, preferred_element_type=F32)
             + bias_ref[0, h, :, g * NA_Q:(g + 1) * NA_Q]
             for g, kr in enumerate(k_refs)]
        s_meta = lax.dot_general(q, km_ref[:, sl], _NT, preferred_element_type=F32)
        m = jnp.max(s_meta, axis=-1, keepdims=True)
        for sg in s:
            m = jnp.maximum(m, jnp.max(sg, axis=-1, keepdims=True))
        p_meta = jnp.exp(s_meta - m)
        l = jnp.sum(p_meta, axis=-1, keepdims=True)
        o = jnp.dot(p_meta.astype(BF16), vm_ref[:, sl], preferred_element_type=F32)
        for sg, vr in zip(s, v_refs):
            p = jnp.exp(sg - m)
            l = l + jnp.sum(p, axis=-1, keepdims=True)
            o = o + jnp.dot(p.astype(BF16), vr[0, :, sl], preferred_element_type=F32)
        o_ref[0, :, sl] = (o / l).astype(BF16)


def _na_attend(proj, meta_proj, bias_tabs):
    b, n, _ = proj.shape
    assert n % NA_Q == 0
    nblk = n // NA_Q
    assert nblk >= 3
    cq, ck, cv = COL_QA // ATT_WIDTH, COL_KA // ATT_WIDTH, COL_VA // ATT_WIDTH

    def seg(col, shift):
        return pl.BlockSpec(
            (1, NA_Q, ATT_WIDTH),
            lambda i, bb: (bb, jnp.clip(i + shift, 0, nblk - 1), col))

    def step_kind(i, bb):
        return (jnp.where(i == 0, 0, jnp.where(i == nblk - 1, 2, 1)), 0, 0, 0)

    return pl.pallas_call(
        _na_body,
        grid=(nblk, b),
        in_specs=[seg(cq, 0),
                  seg(ck, -1), seg(ck, 0), seg(ck, 1),
                  seg(cv, -1), seg(cv, 0), seg(cv, 1),
                  pl.BlockSpec((N_META, ATT_WIDTH), lambda i, bb: (0, ck)),
                  pl.BlockSpec((N_META, ATT_WIDTH), lambda i, bb: (0, cv)),
                  pl.BlockSpec((1, HEADS, NA_Q, NA_KEY_SEGS * NA_Q), step_kind)],
        out_specs=pl.BlockSpec((1, NA_Q, ATT_WIDTH), lambda i, bb: (bb, i, 0)),
        out_shape=jax.ShapeDtypeStruct((b, n, ATT_WIDTH), BF16),
        compiler_params=pltpu.CompilerParams(
            dimension_semantics=("arbitrary", "arbitrary"), vmem_limit_bytes=VMEM_LIMIT),
        name="na_attend",
    )(proj, proj, proj, proj, proj, proj, proj, meta_proj, meta_proj, bias_tabs)


def _wa_bias_tables(t5_bias, nb):
    tab = t5_bias.astype(F32)
    qq = jnp.arange(BLOCK)
    kk = jnp.arange(3 * BLOCK)
    rel = kk[None, :] - BLOCK - qq[:, None]
    band = jnp.transpose(tab[_t5_bucket(rel)], (2, 0, 1))
    in_window = jnp.abs(rel) <= WINDOW
    seg = (kk // BLOCK)[None, :]
    kinds = (in_window & (seg != 0), in_window, in_window & (seg != 2))
    band_tabs = jnp.stack([jnp.where(ok[None], band, NEG_INF) for ok in kinds])
    q_pos = N_META + jnp.arange(nb)[:, None] * BLOCK + qq[None, :]
    meta_rel = jnp.arange(N_META)[None, None, :] - q_pos[:, :, None]
    meta_tabs = jnp.transpose(tab[_t5_bucket(meta_rel)], (0, 3, 1, 2))
    return band_tabs, meta_tabs


def _wa_body(sink_ref, q_ref, kp_ref, kc_ref, kn_ref, vp_ref, vc_ref, vn_ref, km_ref, vm_ref,
             band_ref, mbias_ref, o_ref):
    k_refs = (kp_ref, kc_ref, kn_ref)
    v_refs = (vp_ref, vc_ref, vn_ref)
    for h in range(HEADS):
        sl = slice(h * HEAD_DIM, (h + 1) * HEAD_DIM)
        kv = h // WA_GROUP
        ksl = slice(kv * HEAD_DIM, (kv + 1) * HEAD_DIM)
        q = q_ref[0, :, sl]
        s = [lax.dot_general(q, kr[0, :, ksl], _NT, preferred_element_type=F32)
             + band_ref[0, h, :, g * BLOCK:(g + 1) * BLOCK]
             for g, kr in enumerate(k_refs)]
        s_meta = (lax.dot_general(q, km_ref[:, ksl], _NT, preferred_element_type=F32)
                  + mbias_ref[0, h])
        sink = sink_ref[h]
        m = jnp.maximum(jnp.max(s_meta, axis=-1, keepdims=True), sink)
        for sg in s:
            m = jnp.maximum(m, jnp.max(sg, axis=-1, keepdims=True))
        p_meta = jnp.exp(s_meta - m)
        l = jnp.sum(p_meta, axis=-1, keepdims=True) + jnp.exp(sink - m)
        o = jnp.dot(p_meta.astype(BF16), vm_ref[:, ksl], preferred_element_type=F32)
        for sg, vr in zip(s, v_refs):
            p = jnp.exp(sg - m)
            l = l + jnp.sum(p, axis=-1, keepdims=True)
            o = o + jnp.dot(p.astype(BF16), vr[0, :, ksl], preferred_element_type=F32)
        o_ref[0, :, sl] = (o / l).astype(BF16)


def _wa_attend(proj, meta_proj, band_tabs, meta_tabs, sink):
    b, n, _ = proj.shape
    assert n % BLOCK == 0
    nb = n // BLOCK
    assert nb >= 2
    cq = COL_QB // ATT_WIDTH
    ck, cv = COL_KB // WA_KV_WIDTH, COL_VB // WA_KV_WIDTH

    def seg(col, shift):
        return pl.BlockSpec(
            (1, BLOCK, WA_KV_WIDTH),
            lambda i, bb: (bb, jnp.clip(i + shift, 0, nb - 1), col))

    def block_kind(i, bb):
        return (jnp.where(i == 0, 0, jnp.where(i == nb - 1, 2, 1)), 0, 0, 0)

    return pl.pallas_call(
        _wa_body,
        grid=(nb, b),
        in_specs=[pl.BlockSpec(memory_space=pltpu.SMEM),
                  pl.BlockSpec((1, BLOCK, ATT_WIDTH), lambda i, bb: (bb, i, cq)),
                  seg(ck, -1), seg(ck, 0), seg(ck, 1),
                  seg(cv, -1), seg(cv, 0), seg(cv, 1),
                  pl.BlockSpec((N_META, WA_KV_WIDTH), lambda i, bb: (0, ck)),
                  pl.BlockSpec((N_META, WA_KV_WIDTH), lambda i, bb: (0, cv)),
                  pl.BlockSpec((1, HEADS, BLOCK, 3 * BLOCK), block_kind),
                  pl.BlockSpec((1, HEADS, BLOCK, N_META), lambda i, bb: (i, 0, 0, 0))],
        out_specs=pl.BlockSpec((1, BLOCK, ATT_WIDTH), lambda i, bb: (bb, i, 0)),
        out_shape=jax.ShapeDtypeStruct((b, n, ATT_WIDTH), BF16),
        compiler_params=pltpu.CompilerParams(
            dimension_semantics=("arbitrary", "arbitrary"), vmem_limit_bytes=VMEM_LIMIT),
        name="wa_attend",
    )(sink, proj, proj, proj, proj, proj, proj, proj, meta_proj, meta_proj,
      band_tabs, meta_tabs)


def _mix_out_body(x_ref, oa_ref, ob_ref, za_ref, zb_ref, ga_ref, gb_ref,
                  wpa_ref, wpb_ref, wout_ref, fg_ref, y_ref):
    ta = (oa_ref[...].astype(F32) * jax.nn.silu(za_ref[...].astype(F32))).astype(BF16)
    tb = (ob_ref[...].astype(F32) * jax.nn.silu(zb_ref[...].astype(F32))).astype(BF16)
    ya = jnp.dot(ta, wpa_ref[...], preferred_element_type=F32)
    yb = jnp.dot(tb, wpb_ref[...], preferred_element_type=F32)
    merged = (jax.nn.sigmoid(ga_ref[...].astype(F32)) * ya
              + jax.nn.sigmoid(gb_ref[...].astype(F32)) * yb).astype(BF16)
    h = x_ref[...] + jnp.dot(merged, wout_ref[...], preferred_element_type=F32)
    ms = jnp.mean(h * h, axis=-1, keepdims=True)
    y_ref[...] = (h * lax.rsqrt(ms + RMS_EPS)) * fg_ref[...]


def _mix_out(x2d, oa, ob, proj2d, wpa, wpb, wout, fg, tm):
    m = x2d.shape[0]
    assert m % tm == 0

    def cols(width, col):
        return pl.BlockSpec((tm, width), lambda i: (i, col // width))

    def whole(shape):
        return pl.BlockSpec(shape, lambda i: (0, 0))

    return pl.pallas_call(
        _mix_out_body,
        grid=(m // tm,),
        in_specs=[cols(D_MODEL, 0), cols(ATT_WIDTH, 0), cols(ATT_WIDTH, 0),
                  cols(ATT_WIDTH, COL_ZA), cols(ATT_WIDTH, COL_ZB),
                  cols(D_MODEL, COL_GA), cols(D_MODEL, COL_GB),
                  whole((ATT_WIDTH, D_MODEL)), whole((ATT_WIDTH, D_MODEL)),
                  whole((D_MODEL, D_MODEL)), whole((1, D_MODEL))],
        out_specs=cols(D_MODEL, 0),
        out_shape=jax.ShapeDtypeStruct((m, D_MODEL), F32),
        compiler_params=pltpu.CompilerParams(
            dimension_semantics=("arbitrary",), vmem_limit_bytes=VMEM_LIMIT),
        name="mix_out",
    )(x2d, oa, ob, proj2d, proj2d, proj2d, proj2d, wpa, wpb, wout, fg)


def _encode(x, meta_proj, params):
    b, n, _ = x.shape
    x2d = x.reshape(b * n, D_MODEL)
    proj2d = _norm_proj(x2d, params["norm_g"], params["w_in"], PROJ_TM)
    proj = proj2d.reshape(b, n, IN_WIDTH)
    oa = _na_attend(proj, meta_proj, params["na_bias"])
    band_tabs, meta_tabs = _wa_bias_tables(params["t5_bias"], n // BLOCK)
    ob = _wa_attend(proj, meta_proj, band_tabs, meta_tabs, params["sink"])
    y = _mix_out(x2d, oa.reshape(b * n, ATT_WIDTH), ob.reshape(b * n, ATT_WIDTH), proj2d,
                 params["w_proj_a"], params["w_proj_b"], params["w_out"], params["final_g"],
                 MIX_TM)
    return y.reshape(b, n, D_MODEL)


def kernel(x_prompt, x_sample, meta_tokens, norm_g, w_in, na_rpb, sink_logit, w_proj_a, w_proj_b,
           w_out, t5_bias, final_g):
    assert norm_g.shape[0] == 1, "one layer"
    w = w_in[0]
    scale = HEAD_DIM ** -0.5
    w = jnp.concatenate([w[:, 0:512] * scale, w[:, 512:2048], w[:, 2048:2560] * scale,
                         w[:, 2816:3328], w[:, 3328:5376], w[:, 2560:2816]], axis=1)
    params = {
        "norm_g": norm_g[0].reshape(1, D_MODEL).astype(F32),
        "w_in": w.astype(BF16),
        "na_bias": _na_bias_tables(na_rpb[0]),
        "t5_bias": t5_bias,
        "sink": sink_logit[0].astype(F32),
        "w_proj_a": w_proj_a[0].astype(BF16),
        "w_proj_b": w_proj_b[0].astype(BF16),
        "w_out": w_out[0].astype(BF16),
        "final_g": final_g.reshape(1, D_MODEL).astype(F32),
    }
    meta_proj = _norm_proj(meta_tokens.astype(F32), params["norm_g"], params["w_in"], N_META)
    return (_encode(x_prompt, meta_proj, params), _encode(x_sample, meta_proj, params))
```

```python
import functools
import math

import jax
import jax.numpy as jnp
from jax import lax
from jax.experimental import pallas as pl
from jax.experimental.pallas import tpu as pltpu

F32 = jnp.float32
BF16 = jnp.bfloat16

D_MODEL = 1024
N_META = 16
GRID_W = 64
HEADS = 8
HEAD_DIM = 64
ATT_WIDTH = HEADS * HEAD_DIM
NA_WIN_ROWS = 8
NA_WIN_COLS = 16
WA_KV_HEADS = 2
WA_GROUP = HEADS // WA_KV_HEADS
WA_KV_WIDTH = WA_KV_HEADS * HEAD_DIM
WINDOW = 128
BLOCK = 128
T5_BUCKETS = 32
T5_MAX_DIST = 128
RMS_EPS = 1e-6
NEG_INF = -1e30
IN_WIDTH = 5376

COL_QA, COL_KA, COL_VA, COL_ZA, COL_QB, COL_ZB = 0, 512, 1024, 1536, 2048, 2560
COL_GA, COL_GB, COL_KB, COL_VB = 3072, 4096, 5120, 5248

NA_ROWS_PER_STEP = 4
NA_Q = NA_ROWS_PER_STEP * GRID_W
NA_KEY_SEGS = 3
PROJ_TM = 512
MIX_TM = 512
PROJ_N_CHUNK = 768
VMEM_LIMIT = 52 * 1024 * 1024

_NT = (((1,), (1,)), ((), ()))


def _t5_bucket(rel):
    half = T5_BUCKETS // 2
    exact = half // 2
    ret = jnp.where(rel > 0, half, 0)
    n = jnp.abs(rel)
    nf = jnp.maximum(n, 1).astype(F32)
    large = exact + (jnp.log(nf / exact) / math.log(T5_MAX_DIST / exact)
                     * (half - exact)).astype(jnp.int32)
    large = jnp.minimum(large, half - 1)
    return ret + jnp.where(n < exact, n, large)


def _t5_lookup(tab, rel):
    bucket = _t5_bucket(rel)
    tail = (1,) * bucket.ndim
    hit = bucket[None] == jnp.arange(T5_BUCKETS).reshape((T5_BUCKETS,) + tail)
    vals = jnp.transpose(tab).reshape((tab.shape[1], T5_BUCKETS) + tail)
    return jnp.sum(jnp.where(hit[None], vals, 0.0), axis=1)


def _toeplitz(v, rows, cols):
    length = rows + cols - 1
    assert v.shape[-1] == length
    lead = v.shape[:-1]
    flat = jnp.broadcast_to(v[..., None, :], lead + (rows, length)).reshape(lead + (rows * length,))
    flat = flat[..., rows - 1:rows - 1 + rows * (length - 1)]
    return flat.reshape(lead + (rows, length - 1))[..., :cols]


def _norm_proj_body(x_ref, g_ref, w_ref, o_ref):
    x = x_ref[...]
    ms = jnp.mean(x * x, axis=-1, keepdims=True)
    u = ((x * lax.rsqrt(ms + RMS_EPS)) * g_ref[...]).astype(BF16)
    for j in range(0, IN_WIDTH, PROJ_N_CHUNK):
        o_ref[:, j:j + PROJ_N_CHUNK] = jnp.dot(
            u, w_ref[:, j:j + PROJ_N_CHUNK], preferred_element_type=F32).astype(BF16)


def _norm_proj(x2d, g, w, tm):
    m = x2d.shape[0]
    assert m % tm == 0
    return pl.pallas_call(
        _norm_proj_body,
        grid=(m // tm,),
        in_specs=[pl.BlockSpec((tm, D_MODEL), lambda i: (i, 0)),
                  pl.BlockSpec((1, D_MODEL), lambda i: (0, 0)),
                  pl.BlockSpec((D_MODEL, IN_WIDTH), lambda i: (0, 0))],
        out_specs=pl.BlockSpec((tm, IN_WIDTH), lambda i: (i, 0)),
        out_shape=jax.ShapeDtypeStruct((m, IN_WIDTH), BF16),
        compiler_params=pltpu.CompilerParams(
            dimension_semantics=("arbitrary",), vmem_limit_bytes=VMEM_LIMIT),
        name="norm_proj",
    )(x2d, g, w)


def _na_bias_tables(rpb):
    n_t = NA_KEY_SEGS * NA_ROWS_PER_STEP
    lead = GRID_W - NA_WIN_COLS
    v = jnp.pad(rpb.astype(F32), ((0, 0), (0, 0), (lead, lead)))
    cols = _toeplitz(v, GRID_W, GRID_W)
    j = jnp.arange(GRID_W)[:, None]
    c = jnp.arange(GRID_W)[None, :]
    cs = jnp.clip(j - NA_WIN_COLS // 2, 0, GRID_W - NA_WIN_COLS)
    cols = jnp.where((c >= cs) & (c < cs + NA_WIN_COLS), cols, NEG_INF)
    masked = jnp.full((HEADS, GRID_W, GRID_W), NEG_INF, F32)
    row_ok = (
        lambda a, t: t >= NA_ROWS_PER_STEP,
        lambda a, t: 0 <= t - a < NA_WIN_ROWS,
        lambda a, t: t < NA_WIN_ROWS,
    )
    tabs = []
    for ok in row_ok:
        per_a = [jnp.stack([cols[:, t - a + NA_WIN_ROWS // 2 - 1] if ok(a, t) else masked
                            for t in range(n_t)], axis=2)
                 for a in range(NA_ROWS_PER_STEP)]
        tabs.append(jnp.stack(per_a, axis=1))
    return jnp.stack(tabs).reshape(3, HEADS, NA_Q, NA_KEY_SEGS * NA_Q)


def _na_body(q_ref, kp_ref, kc_ref, kn_ref, vp_ref, vc_ref, vn_ref, km_ref, vm_ref,
             bias_ref, o_ref):
    k_refs = (kp_ref, kc_ref, kn_ref)
    v_refs = (vp_ref, vc_ref, vn_ref)
    for h in range(HEADS):
        sl = slice(h * HEAD_DIM, (h + 1) * HEAD_DIM)
        q = q_ref[0, :, sl]
        s = [lax.dot_general(q, kr[0, :, sl], _NT, preferred_element_type=F32)
             + bias_ref[0, h, :, g * NA_Q:(g + 1) * NA_Q]
             for g, kr in enumerate(k_refs)]
        s_meta = lax.dot_general(q, km_ref[:, sl], _NT, preferred_element_type=F32)
        m = jnp.max(s_meta, axis=-1, keepdims=True)
        for sg in s:
            m = jnp.maximum(m, jnp.max(sg, axis=-1, keepdims=True))
        p_meta = jnp.exp(s_meta - m)
        l = jnp.sum(p_meta, axis=-1, keepdims=True)
        o = jnp.dot(p_meta.astype(BF16), vm_ref[:, sl], preferred_element_type=F32)
        for sg, vr in zip(s, v_refs):
            p = jnp.exp(sg - m)
            l = l + jnp.sum(p, axis=-1, keepdims=True)
            o = o + jnp.dot(p.astype(BF16), vr[0, :, sl], preferred_element_type=F32)
        o_ref[0, :, sl] = (o / l).astype(BF16)


def _na_attend(proj, meta_proj, bias_tabs):
    b, n, _ = proj.shape
    assert n % NA_Q == 0
    nblk = n // NA_Q
    assert nblk >= 3
    cq, ck, cv = COL_QA // ATT_WIDTH, COL_KA // ATT_WIDTH, COL_VA // ATT_WIDTH

    def seg(col, shift):
        return pl.BlockSpec(
            (1, NA_Q, ATT_WIDTH),
            lambda i, bb: (bb, jnp.clip(i + shift, 0, nblk - 1), col))

    def step_kind(i, bb):
        return (jnp.where(i == 0, 0, jnp.where(i == nblk - 1, 2, 1)), 0, 0, 0)

    return pl.pallas_call(
        _na_body,
        grid=(nblk, b),
        in_specs=[seg(cq, 0),
                  seg(ck, -1), seg(ck, 0), seg(ck, 1),
                  seg(cv, -1), seg(cv, 0), seg(cv, 1),
                  pl.BlockSpec((N_META, ATT_WIDTH), lambda i, bb: (0, ck)),
                  pl.BlockSpec((N_META, ATT_WIDTH), lambda i, bb: (0, cv)),
                  pl.BlockSpec((1, HEADS, NA_Q, NA_KEY_SEGS * NA_Q), step_kind)],
        out_specs=pl.BlockSpec((1, NA_Q, ATT_WIDTH), lambda i, bb: (bb, i, 0)),
        out_shape=jax.ShapeDtypeStruct((b, n, ATT_WIDTH), BF16),
        compiler_params=pltpu.CompilerParams(
            dimension_semantics=("arbitrary", "arbitrary"), vmem_limit_bytes=VMEM_LIMIT),
        name="na_attend",
    )(proj, proj, proj, proj, proj, proj, proj, meta_proj, meta_proj, bias_tabs)


def _wa_bias_tables(t5_bias, nb):
    tab = t5_bias.astype(F32)
    qq = jnp.arange(BLOCK)
    kk = jnp.arange(3 * BLOCK)
    rel = kk[None, :] - BLOCK - qq[:, None]
    dist = jnp.arange(-(2 * BLOCK - 1), 2 * BLOCK)
    band = _toeplitz(_t5_lookup(tab, dist), BLOCK, 3 * BLOCK)
    in_window = jnp.abs(rel) <= WINDOW
    seg = (kk // BLOCK)[None, :]
    kinds = (in_window & (seg != 0), in_window, in_window & (seg != 2))
    band_tabs = jnp.stack([jnp.where(ok[None], band, NEG_INF) for ok in kinds])
    q_pos = N_META + jnp.arange(nb)[:, None] * BLOCK + qq[None, :]
    meta_rel = jnp.arange(N_META)[None, None, :] - q_pos[:, :, None]
    meta_tabs = _t5_lookup(tab, meta_rel)
    return band_tabs, jnp.transpose(meta_tabs, (1, 0, 2, 3))


def _wa_body(sink_ref, q_ref, kp_ref, kc_ref, kn_ref, vp_ref, vc_ref, vn_ref, km_ref, vm_ref,
             band_ref, mbias_ref, o_ref):
    k_refs = (kp_ref, kc_ref, kn_ref)
    v_refs = (vp_ref, vc_ref, vn_ref)
    for h in range(HEADS):
        sl = slice(h * HEAD_DIM, (h + 1) * HEAD_DIM)
        kv = h // WA_GROUP
        ksl = slice(kv * HEAD_DIM, (kv + 1) * HEAD_DIM)
        q = q_ref[0, :, sl]
        s = [lax.dot_general(q, kr[0, :, ksl], _NT, preferred_element_type=F32)
             + band_ref[0, h, :, g * BLOCK:(g + 1) * BLOCK]
             for g, kr in enumerate(k_refs)]
        s_meta = (lax.dot_general(q, km_ref[:, ksl], _NT, preferred_element_type=F32)
                  + mbias_ref[0, h])
        sink = sink_ref[h]
        m = jnp.maximum(jnp.max(s_meta, axis=-1, keepdims=True), sink)
        for sg in s:
            m = jnp.maximum(m, jnp.max(sg, axis=-1, keepdims=True))
        p_meta = jnp.exp(s_meta - m)
        l = jnp.sum(p_meta, axis=-1, keepdims=True) + jnp.exp(sink - m)
        o = jnp.dot(p_meta.astype(BF16), vm_ref[:, ksl], preferred_element_type=F32)
        for sg, vr in zip(s, v_refs):
            p = jnp.exp(sg - m)
            l = l + jnp.sum(p, axis=-1, keepdims=True)
            o = o + jnp.dot(p.astype(BF16), vr[0, :, ksl], preferred_element_type=F32)
        o_ref[0, :, sl] = (o / l).astype(BF16)


def _wa_attend(proj, meta_proj, band_tabs, meta_tabs, sink):
    b, n, _ = proj.shape
    assert n % BLOCK == 0
    nb = n // BLOCK
    assert nb >= 2
    cq = COL_QB // ATT_WIDTH
    ck, cv = COL_KB // WA_KV_WIDTH, COL_VB // WA_KV_WIDTH

    def seg(col, shift):
        return pl.BlockSpec(
            (1, BLOCK, WA_KV_WIDTH),
            lambda i, bb: (bb, jnp.clip(i + shift, 0, nb - 1), col))

    def block_kind(i, bb):
        return (jnp.where(i == 0, 0, jnp.where(i == nb - 1, 2, 1)), 0, 0, 0)

    return pl.pallas_call(
        _wa_body,
        grid=(nb, b),
        in_specs=[pl.BlockSpec(memory_space=pltpu.SMEM),
                  pl.BlockSpec((1, BLOCK, ATT_WIDTH), lambda i, bb: (bb, i, cq)),
                  seg(ck, -1), seg(ck, 0), seg(ck, 1),
                  seg(cv, -1), seg(cv, 0), seg(cv, 1),
                  pl.BlockSpec((N_META, WA_KV_WIDTH), lambda i, bb: (0, ck)),
                  pl.BlockSpec((N_META, WA_KV_WIDTH), lambda i, bb: (0, cv)),
                  pl.BlockSpec((1, HEADS, BLOCK, 3 * BLOCK), block_kind),
                  pl.BlockSpec((1, HEADS, BLOCK, N_META), lambda i, bb: (i, 0, 0, 0))],
        out_specs=pl.BlockSpec((1, BLOCK, ATT_WIDTH), lambda i, bb: (bb, i, 0)),
        out_shape=jax.ShapeDtypeStruct((b, n, ATT_WIDTH), BF16),
        compiler_params=pltpu.CompilerParams(
            dimension_semantics=("arbitrary", "arbitrary"), vmem_limit_bytes=VMEM_LIMIT),
        name="wa_attend",
    )(sink, proj, proj, proj, proj, proj, proj, proj, meta_proj, meta_proj,
      band_tabs, meta_tabs)


def _mix_out_body(x_ref, oa_ref, ob_ref, za_ref, zb_ref, ga_ref, gb_ref,
                  wpa_ref, wpb_ref, wout_ref, fg_ref, y_ref):
    ta = (oa_ref[...].astype(F32) * jax.nn.silu(za_ref[...].astype(F32))).astype(BF16)
    tb = (ob_ref[...].astype(F32) * jax.nn.silu(zb_ref[...].astype(F32))).astype(BF16)
    ya = jnp.dot(ta, wpa_ref[...], preferred_element_type=F32)
    yb = jnp.dot(tb, wpb_ref[...], preferred_element_type=F32)
    merged = (jax.nn.sigmoid(ga_ref[...].astype(F32)) * ya
              + jax.nn.sigmoid(gb_ref[...].astype(F32)) * yb).astype(BF16)
    h = x_ref[...] + jnp.dot(merged, wout_ref[...], preferred_element_type=F32)
    ms = jnp.mean(h * h, axis=-1, keepdims=True)
    y_ref[...] = (h * lax.rsqrt(ms + RMS_EPS)) * fg_ref[...]


def _mix_out(x2d, oa, ob, proj2d, wpa, wpb, wout, fg, tm):
    m = x2d.shape[0]
    assert m % tm == 0

    def cols(width, col):
        return pl.BlockSpec((tm, width), lambda i: (i, col // width))

    def whole(shape):
        return pl.BlockSpec(shape, lambda i: (0, 0))

    return pl.pallas_call(
        _mix_out_body,
        grid=(m // tm,),
        in_specs=[cols(D_MODEL, 0), cols(ATT_WIDTH, 0), cols(ATT_WIDTH, 0),
                  cols(ATT_WIDTH, COL_ZA), cols(ATT_WIDTH, COL_ZB),
                  cols(D_MODEL, COL_GA), cols(D_MODEL, COL_GB),
                  whole((ATT_WIDTH, D_MODEL)), whole((ATT_WIDTH, D_MODEL)),
                  whole((D_MODEL, D_MODEL)), whole((1, D_MODEL))],
        out_specs=cols(D_MODEL, 0),
        out_shape=jax.ShapeDtypeStruct((m, D_MODEL), F32),
        compiler_params=pltpu.CompilerParams(
            dimension_semantics=("arbitrary",), vmem_limit_bytes=VMEM_LIMIT),
        name="mix_out",
    )(x2d, oa, ob, proj2d, proj2d, proj2d, proj2d, wpa, wpb, wout, fg)


def _encode(x, meta_proj, params):
    b, n, _ = x.shape
    x2d = x.reshape(b * n, D_MODEL)
    proj2d = _norm_proj(x2d, params["norm_g"], params["w_in"], PROJ_TM)
    proj = proj2d.reshape(b, n, IN_WIDTH)
    oa = _na_attend(proj, meta_proj, params["na_bias"])
    band_tabs, meta_tabs = _wa_bias_tables(params["t5_bias"], n // BLOCK)
    ob = _wa_attend(proj, meta_proj, band_tabs, meta_tabs, params["sink"])
    y = _mix_out(x2d, oa.reshape(b * n, ATT_WIDTH), ob.reshape(b * n, ATT_WIDTH), proj2d,
                 params["w_proj_a"], params["w_proj_b"], params["w_out"], params["final_g"],
                 MIX_TM)
    return y.reshape(b, n, D_MODEL)


def kernel(x_prompt, x_sample, meta_tokens, norm_g, w_in, na_rpb, sink_logit, w_proj_a, w_proj_b,
           w_out, t5_bias, final_g):
    assert norm_g.shape[0] == 1, "one layer"
    w = w_in[0]
    scale = HEAD_DIM ** -0.5
    w = jnp.concatenate([w[:, 0:512] * scale, w[:, 512:2048], w[:, 2048:2560] * scale,
                         w[:, 2816:3328], w[:, 3328:5376], w[:, 2560:2816]], axis=1)
    params = {
        "norm_g": norm_g[0].reshape(1, D_MODEL).astype(F32),
        "w_in": w.astype(BF16),
        "na_bias": _na_bias_tables(na_rpb[0]),
        "t5_bias": t5_bias,
        "sink": sink_logit[0].astype(F32),
        "w_proj_a": w_proj_a[0].astype(BF16),
        "w_proj_b": w_proj_b[0].astype(BF16),
        "w_out": w_out[0].astype(BF16),
        "final_g": final_g.reshape(1, D_MODEL).astype(F32),
    }
    meta_proj = _norm_proj(meta_tokens.astype(F32), params["norm_g"], params["w_in"], N_META)
    return (_encode(x_prompt, meta_proj, params), _encode(x_sample, meta_proj, params))
```

```python
import math

import jax
import jax.numpy as jnp
from jax import lax
from jax.experimental import pallas as pl
from jax.experimental.pallas import tpu as pltpu

F32 = jnp.float32
BF16 = jnp.bfloat16

D_MODEL = 1024
N_META = 16
GRID_W = 64
HEADS = 8
HEAD_DIM = 64
ATT_WIDTH = HEADS * HEAD_DIM
NA_WIN_ROWS = 8
NA_WIN_COLS = 16
WA_KV_HEADS = 2
WA_GROUP = HEADS // WA_KV_HEADS
WA_KV_WIDTH = WA_KV_HEADS * HEAD_DIM
WINDOW = 128
BLOCK = 128
T5_BUCKETS = 32
T5_MAX_DIST = 128
RMS_EPS = 1e-6
NEG_INF = -1e30
LOG2_E = math.log2(math.e)

K_WIDTH = ATT_WIDTH + WA_KV_WIDTH
ROW_GA, ROW_GB, ROW_QA, ROW_VA, ROW_ZA, ROW_QB, ROW_ZB, ROW_VB = (
    0, 1024, 2048, 2560, 3072, 3584, 4096, 4608)
T_WIDTH = 4736

NA_ROWS_PER_STEP = 4
NA_Q = NA_ROWS_PER_STEP * GRID_W
NA_KEY_SEGS = 3
PAIR = 2 * HEAD_DIM
PROJ_TM = 512
PROJ_ROW_CHUNK = 1184
MIX_TM = 512
VMEM_LIMIT = 52 * 1024 * 1024

_NT = (((1,), (1,)), ((), ()))
_TN = (((0,), (0,)), ((), ()))


def _t5_bucket(rel):
    half = T5_BUCKETS // 2
    exact = half // 2
    ret = jnp.where(rel > 0, half, 0)
    n = jnp.abs(rel)
    nf = jnp.maximum(n, 1).astype(F32)
    large = exact + (jnp.log(nf / exact) / math.log(T5_MAX_DIST / exact)
                     * (half - exact)).astype(jnp.int32)
    large = jnp.minimum(large, half - 1)
    return ret + jnp.where(n < exact, n, large)


def _t5_lookup(tab, rel):
    bucket = _t5_bucket(rel)
    tail = (1,) * bucket.ndim
    hit = bucket[None] == jnp.arange(T5_BUCKETS).reshape((T5_BUCKETS,) + tail)
    vals = jnp.transpose(tab).reshape((tab.shape[1], T5_BUCKETS) + tail)
    return jnp.sum(jnp.where(hit[None], vals, 0.0), axis=1)


def _toeplitz(v, rows, cols):
    length = rows + cols - 1
    assert v.shape[-1] == length
    lead = v.shape[:-1]
    flat = jnp.broadcast_to(v[..., None, :], lead + (rows, length)).reshape(lead + (rows * length,))
    flat = flat[..., rows - 1:rows - 1 + rows * (length - 1)]
    return flat.reshape(lead + (rows, length - 1))[..., :cols]


def _half_padded(q, upper):
    zeros = jnp.zeros_like(q)
    return jnp.concatenate([zeros, q] if upper else [q, zeros], axis=0)


def _norm_proj_body(x_ref, g_ref, wk_ref, wt_ref, ok_ref, ot_ref):
    x = x_ref[...]
    ms = jnp.mean(x * x, axis=-1, keepdims=True)
    u = ((x * lax.rsqrt(ms + RMS_EPS)) * g_ref[...]).astype(BF16)
    ok_ref[...] = jnp.dot(u, wk_ref[...], preferred_element_type=F32).astype(BF16)
    for r in range(0, T_WIDTH, PROJ_ROW_CHUNK):
        ot_ref[0, r:r + PROJ_ROW_CHUNK, :] = lax.dot_general(
            wt_ref[r:r + PROJ_ROW_CHUNK, :], u, _NT, preferred_element_type=F32).astype(BF16)


def _norm_proj(x, g, wk, wt, tm):
    b, n, _ = x.shape
    assert n % tm == 0
    return pl.pallas_call(
        _norm_proj_body,
        grid=(b, n // tm),
        in_specs=[pl.BlockSpec((None, tm, D_MODEL), lambda bb, i: (bb, i, 0)),
                  pl.BlockSpec((1, D_MODEL), lambda bb, i: (0, 0)),
                  pl.BlockSpec((D_MODEL, K_WIDTH), lambda bb, i: (0, 0)),
                  pl.BlockSpec((T_WIDTH, D_MODEL), lambda bb, i: (0, 0))],
        out_specs=[pl.BlockSpec((None, tm, K_WIDTH), lambda bb, i: (bb, i, 0)),
                   pl.BlockSpec((1, T_WIDTH, tm), lambda bb, i: (bb, 0, i))],
        out_shape=[jax.ShapeDtypeStruct((b, n, K_WIDTH), BF16),
                   jax.ShapeDtypeStruct((b, T_WIDTH, n), BF16)],
        compiler_params=pltpu.CompilerParams(
            dimension_semantics=("arbitrary", "arbitrary"), vmem_limit_bytes=VMEM_LIMIT),
        name="norm_proj",
    )(x, g, wk, wt)


def _na_bias_tables(rpb):
    n_t = NA_KEY_SEGS * NA_ROWS_PER_STEP
    lead = GRID_W - NA_WIN_COLS
    v = jnp.pad(rpb.astype(F32) * LOG2_E, ((0, 0), (0, 0), (lead, lead)))
    cols = _toeplitz(v, GRID_W, GRID_W)
    j = jnp.arange(GRID_W)[:, None]
    c = jnp.arange(GRID_W)[None, :]
    cs = jnp.clip(j - NA_WIN_COLS // 2, 0, GRID_W - NA_WIN_COLS)
    cols = jnp.where((c >= cs) & (c < cs + NA_WIN_COLS), cols, NEG_INF)
    cols_t = jnp.swapaxes(cols, -1, -2)
    masked = jnp.full((HEADS, GRID_W, GRID_W), NEG_INF, F32)
    row_ok = (
        lambda a, t: t >= NA_ROWS_PER_STEP,
        lambda a, t: 0 <= t - a < NA_WIN_ROWS,
        lambda a, t: t < NA_WIN_ROWS,
    )
    tabs = []
    for ok in row_ok:
        per_t = [jnp.stack([cols_t[:, t - a + NA_WIN_ROWS // 2 - 1] if ok(a, t) else masked
                            for a in range(NA_ROWS_PER_STEP)], axis=2)
                 for t in range(n_t)]
        tabs.append(jnp.stack(per_t, axis=1))
    return jnp.stack(tabs).reshape(3, HEADS, NA_KEY_SEGS * NA_Q, NA_Q)


def _na_body(q_ref, kp_ref, kc_ref, kn_ref, vp_ref, vc_ref, vn_ref, km_ref, vm_ref,
             bias_ref, o_ref):
    k_refs = (kp_ref, kc_ref, kn_ref)
    v_refs = (vp_ref, vc_ref, vn_ref)
    for h in range(HEADS):
        rows = slice(h * HEAD_DIM, (h + 1) * HEAD_DIM)
        slab = slice((h // 2) * PAIR, (h // 2 + 1) * PAIR)
        q = _half_padded(q_ref[0, rows, :], h % 2)
        s = [jnp.dot(kr[0, :, slab], q, preferred_element_type=F32)
             + bias_ref[0, h, g * NA_Q:(g + 1) * NA_Q, :]
             for g, kr in enumerate(k_refs)]
        s_meta = jnp.dot(km_ref[:, slab], q, preferred_element_type=F32)
        m = jnp.max(s_meta, axis=0, keepdims=True)
        for sg in s:
            m = jnp.maximum(m, jnp.max(sg, axis=0, keepdims=True))
        p_meta = jnp.exp2(s_meta - m)
        l = jnp.sum(p_meta, axis=0, keepdims=True)
        o = jnp.dot(vm_ref[0, rows, :], p_meta.astype(BF16), preferred_element_type=F32)
        for sg, vr in zip(s, v_refs):
            p = jnp.exp2(sg - m)
            l = l + jnp.sum(p, axis=0, keepdims=True)
            o = o + jnp.dot(vr[0, rows, :], p.astype(BF16), preferred_element_type=F32)
        o_ref[0, rows, :] = (o / l).astype(BF16)


def _na_attend(pk, pt, meta_pk, meta_pt, bias_tabs):
    b, n, _ = pk.shape
    assert n % NA_Q == 0
    nblk = n // NA_Q
    assert nblk >= 3
    rq, rv = ROW_QA // ATT_WIDTH, ROW_VA // ATT_WIDTH

    def shifted(i, shift):
        return jnp.clip(i + shift, 0, nblk - 1)

    def kseg(shift):
        return pl.BlockSpec((1, NA_Q, ATT_WIDTH), lambda i, bb: (bb, shifted(i, shift), 0))

    def vseg(shift):
        return pl.BlockSpec((1, ATT_WIDTH, NA_Q), lambda i, bb: (bb, rv, shifted(i, shift)))

    def step_kind(i, bb):
        return (jnp.where(i == 0, 0, jnp.where(i == nblk - 1, 2, 1)), 0, 0, 0)

    return pl.pallas_call(
        _na_body,
        grid=(nblk, b),
        in_specs=[pl.BlockSpec((1, ATT_WIDTH, NA_Q), lambda i, bb: (bb, rq, i)),
                  kseg(-1), kseg(0), kseg(1),
                  vseg(-1), vseg(0), vseg(1),
                  pl.BlockSpec((N_META, ATT_WIDTH), lambda i, bb: (0, 0)),
                  pl.BlockSpec((1, ATT_WIDTH, N_META), lambda i, bb: (0, rv, 0)),
                  pl.BlockSpec((1, HEADS, NA_KEY_SEGS * NA_Q, NA_Q), step_kind)],
        out_specs=pl.BlockSpec((1, ATT_WIDTH, NA_Q), lambda i, bb: (bb, 0, i)),
        out_shape=jax.ShapeDtypeStruct((b, ATT_WIDTH, n), BF16),
        compiler_params=pltpu.CompilerParams(
            dimension_semantics=("arbitrary", "arbitrary"), vmem_limit_bytes=VMEM_LIMIT),
        name="na_attend",
    )(pt, pk, pk, pk, pt, pt, pt, meta_pk, meta_pt, bias_tabs)


def _wa_bias_tables(t5_bias, sink, nb):
    tab = t5_bias.astype(F32) * LOG2_E
    qq = jnp.arange(BLOCK)
    kk = jnp.arange(3 * BLOCK)
    rel = kk[None, :] - BLOCK - qq[:, None]
    dist = jnp.arange(-(2 * BLOCK - 1), 2 * BLOCK)
    band = _toeplitz(_t5_lookup(tab, dist), BLOCK, 3 * BLOCK)
    in_window = jnp.abs(rel) <= WINDOW
    seg = (kk // BLOCK)[None, :]
    kinds = (in_window & (seg != 0), in_window, in_window & (seg != 2))
    band = jnp.stack([jnp.where(ok[None], band, NEG_INF) for ok in kinds])
    band = band.reshape(3, WA_KV_HEADS, WA_GROUP, BLOCK, 3 * BLOCK)
    band = jnp.transpose(band, (0, 1, 4, 2, 3)).reshape(3, WA_KV_HEADS, 3 * BLOCK, WA_GROUP * BLOCK)
    q_pos = N_META + jnp.arange(nb)[:, None] * BLOCK + qq[None, :]
    meta_rel = jnp.arange(N_META)[None, None, :] - q_pos[:, :, None]
    meta = _t5_lookup(tab, meta_rel).reshape(WA_KV_HEADS, WA_GROUP, nb, BLOCK, N_META)
    meta = jnp.transpose(meta, (2, 0, 4, 1, 3)).reshape(nb, WA_KV_HEADS, N_META, WA_GROUP * BLOCK)
    sink_row = jnp.broadcast_to((sink.astype(F32) * LOG2_E).reshape(WA_KV_HEADS, WA_GROUP, 1),
                                (WA_KV_HEADS, WA_GROUP, BLOCK)).reshape(WA_KV_HEADS, 1, WA_GROUP * BLOCK)
    return band, meta, sink_row


def _wa_body(q_ref, kp_ref, kc_ref, kn_ref, vp_ref, vc_ref, vn_ref, km_ref, vm_ref,
             band_ref, mbias_ref, sink_ref, o_ref):
    k_refs = (kp_ref, kc_ref, kn_ref)
    v_refs = (vp_ref, vc_ref, vn_ref)
    for kv in range(WA_KV_HEADS):
        vrows = slice(kv * HEAD_DIM, (kv + 1) * HEAD_DIM)
        q4 = jnp.concatenate(
            [q_ref[0, (kv * WA_GROUP + g) * HEAD_DIM:(kv * WA_GROUP + g + 1) * HEAD_DIM, :]
             for g in range(WA_GROUP)], axis=1)
        q4 = _half_padded(q4, kv)
        s = [jnp.dot(kr[0], q4, preferred_element_type=F32)
             + band_ref[0, kv, g * BLOCK:(g + 1) * BLOCK, :]
             for g, kr in enumerate(k_refs)]
        s_meta = jnp.dot(km_ref[...], q4, preferred_element_type=F32) + mbias_ref[0, kv]
        sink = sink_ref[kv]
        m = jnp.maximum(jnp.max(s_meta, axis=0, keepdims=True), sink)
        for sg in s:
            m = jnp.maximum(m, jnp.max(sg, axis=0, keepdims=True))
        p_meta = jnp.exp2(s_meta - m)
        l = jnp.sum(p_meta, axis=0, keepdims=True) + jnp.exp2(sink - m)
        o = jnp.dot(vm_ref[0, vrows, :], p_meta.astype(BF16), preferred_element_type=F32)
        for sg, vr in zip(s, v_refs):
            p = jnp.exp2(sg - m)
            l = l + jnp.sum(p, axis=0, keepdims=True)
            o = o + jnp.dot(vr[0, vrows, :], p.astype(BF16), preferred_element_type=F32)
        o = (o / l).astype(BF16)
        for g in range(WA_GROUP):
            h = kv * WA_GROUP + g
            o_ref[0, h * HEAD_DIM:(h + 1) * HEAD_DIM, :] = o[:, g * BLOCK:(g + 1) * BLOCK]


def _wa_attend(pk, pt, meta_pk, meta_pt, band, meta_bias, sink_row):
    b, n, _ = pk.shape
    assert n % BLOCK == 0
    nb = n // BLOCK
    assert nb >= 2
    rq, rv = ROW_QB // ATT_WIDTH, ROW_VB // WA_KV_WIDTH
    ck = ATT_WIDTH // WA_KV_WIDTH

    def shifted(i, shift):
        return jnp.clip(i + shift, 0, nb - 1)

    def kseg(shift):
        return pl.BlockSpec((1, BLOCK, WA_KV_WIDTH), lambda i, bb: (bb, shifted(i, shift), ck))

    def vseg(shift):
        return pl.BlockSpec((1, WA_KV_WIDTH, BLOCK), lambda i, bb: (bb, rv, shifted(i, shift)))

    def block_kind(i, bb):
        return (jnp.where(i == 0, 0, jnp.where(i == nb - 1, 2, 1)), 0, 0, 0)

    lanes = WA_GROUP * BLOCK
    return pl.pallas_call(
        _wa_body,
        grid=(nb, b),
        in_specs=[pl.BlockSpec((1, ATT_WIDTH, BLOCK), lambda i, bb: (bb, rq, i)),
                  kseg(-1), kseg(0), kseg(1),
                  vseg(-1), vseg(0), vseg(1),
                  pl.BlockSpec((N_META, WA_KV_WIDTH), lambda i, bb: (0, ck)),
                  pl.BlockSpec((1, WA_KV_WIDTH, N_META), lambda i, bb: (0, rv, 0)),
                  pl.BlockSpec((1, WA_KV_HEADS, 3 * BLOCK, lanes), block_kind),
                  pl.BlockSpec((1, WA_KV_HEADS, N_META, lanes), lambda i, bb: (i, 0, 0, 0)),
                  pl.BlockSpec((WA_KV_HEADS, 1, lanes), lambda i, bb: (0, 0, 0))],
        out_specs=pl.BlockSpec((1, ATT_WIDTH, BLOCK), lambda i, bb: (bb, 0, i)),
        out_shape=jax.ShapeDtypeStruct((b, ATT_WIDTH, n), BF16),
        compiler_params=pltpu.CompilerParams(
            dimension_semantics=("arbitrary", "arbitrary"), vmem_limit_bytes=VMEM_LIMIT),
        name="wa_attend",
    )(pt, pk, pk, pk, pt, pt, pt, meta_pk, meta_pt, band, meta_bias, sink_row)


def _mix_out_body(x_ref, oa_ref, ob_ref, za_ref, zb_ref, ga_ref, gb_ref,
                  wpa_ref, wpb_ref, wout_ref, fg_ref, y_ref):
    ta = (oa_ref[0].astype(F32) * jax.nn.silu(za_ref[0].astype(F32))).astype(BF16)
    tb = (ob_ref[0].astype(F32) * jax.nn.silu(zb_ref[0].astype(F32))).astype(BF16)
    ya = jnp.dot(wpa_ref[...], ta, preferred_element_type=F32)
    yb = jnp.dot(wpb_ref[...], tb, preferred_element_type=F32)
    merged = (jax.nn.sigmoid(ga_ref[0].astype(F32)) * ya
              + jax.nn.sigmoid(gb_ref[0].astype(F32)) * yb).astype(BF16)
    h = x_ref[0] + lax.dot_general(merged, wout_ref[...], _TN, preferred_element_type=F32)
    ms = jnp.mean(h * h, axis=-1, keepdims=True)
    y_ref[0] = (h * lax.rsqrt(ms + RMS_EPS)) * fg_ref[...]


def _mix_out(x, oa, ob, pt, wpa_t, wpb_t, wout, fg, tm):
    b, n, _ = x.shape
    assert n % tm == 0

    def feat(rows, row0):
        return pl.BlockSpec((1, rows, tm), lambda bb, i: (bb, row0 // rows, i))

    def whole(shape):
        return pl.BlockSpec(shape, lambda bb, i: (0, 0))

    tok = pl.BlockSpec((1, tm, D_MODEL), lambda bb, i: (bb, i, 0))
    return pl.pallas_call(
        _mix_out_body,
        grid=(b, n // tm),
        in_specs=[tok, feat(ATT_WIDTH, 0), feat(ATT_WIDTH, 0),
                  feat(ATT_WIDTH, ROW_ZA), feat(ATT_WIDTH, ROW_ZB),
                  feat(D_MODEL, ROW_GA), feat(D_MODEL, ROW_GB),
                  whole((D_MODEL, ATT_WIDTH)), whole((D_MODEL, ATT_WIDTH)),
                  whole((D_MODEL, D_MODEL)), whole((1, D_MODEL))],
        out_specs=tok,
        out_shape=jax.ShapeDtypeStruct((b, n, D_MODEL), F32),
        compiler_params=pltpu.CompilerParams(
            dimension_semantics=("arbitrary", "arbitrary"), vmem_limit_bytes=VMEM_LIMIT),
        name="mix_out",
    )(x, oa, ob, pt, pt, pt, pt, wpa_t, wpb_t, wout, fg)


def _encode(x, meta_pk, meta_pt, params):
    n = x.shape[1]
    pk, pt = _norm_proj(x, params["norm_g"], params["w_k"], params["w_t"], PROJ_TM)
    oa = _na_attend(pk, pt, meta_pk, meta_pt, params["na_bias"])
    band, meta_bias, sink_row = _wa_bias_tables(params["t5_bias"], params["sink"], n // BLOCK)
    ob = _wa_attend(pk, pt, meta_pk, meta_pt, band, meta_bias, sink_row)
    return _mix_out(x, oa, ob, pt, params["w_proj_a_t"], params["w_proj_b_t"], params["w_out"],
                    params["final_g"], MIX_TM)


def kernel(x_prompt, x_sample, meta_tokens, norm_g, w_in, na_rpb, sink_logit, w_proj_a, w_proj_b,
           w_out, t5_bias, final_g):
    assert norm_g.shape[0] == 1, "one layer"
    w = w_in[0]
    scale = HEAD_DIM ** -0.5 * LOG2_E
    q_a, k_a, v_a, z_a = (w[:, i * 512:(i + 1) * 512] for i in range(4))
    q_b, k_b, v_b, z_b = w[:, 2048:2560], w[:, 2560:2688], w[:, 2688:2816], w[:, 2816:3328]
    g_a, g_b = w[:, 3328:4352], w[:, 4352:5376]
    w_t = jnp.concatenate([g_a, g_b, q_a * scale, v_a, z_a, q_b * scale, z_b, v_b], axis=1)
    params = {
        "norm_g": norm_g[0].reshape(1, D_MODEL).astype(F32),
        "w_k": jnp.concatenate([k_a, k_b], axis=1).astype(BF16),
        "w_t": jnp.transpose(w_t).astype(BF16),
        "na_bias": _na_bias_tables(na_rpb[0]),
        "t5_bias": t5_bias,
        "sink": sink_logit[0],
        "w_proj_a_t": jnp.transpose(w_proj_a[0]).astype(BF16),
        "w_proj_b_t": jnp.transpose(w_proj_b[0]).astype(BF16),
        "w_out": w_out[0].astype(BF16),
        "final_g": final_g.reshape(1, D_MODEL).astype(F32),
    }
    meta_pk, meta_pt = _norm_proj(meta_tokens.astype(F32)[None], params["norm_g"],
                                  params["w_k"], params["w_t"], N_META)
    return (_encode(x_prompt, meta_pk[0], meta_pt, params),
            _encode(x_sample, meta_pk[0], meta_pt, params))
```

```python
import math

import jax
import jax.numpy as jnp
from jax import lax
from jax.experimental import pallas as pl
from jax.experimental.pallas import tpu as pltpu

F32 = jnp.float32
BF16 = jnp.bfloat16

D_MODEL = 1024
N_META = 16
GRID_W = 64
HEADS = 8
HEAD_DIM = 64
ATT_WIDTH = HEADS * HEAD_DIM
NA_WIN_ROWS = 8
NA_WIN_COLS = 16
WA_KV_HEADS = 2
WA_GROUP = HEADS // WA_KV_HEADS
WA_KV_WIDTH = WA_KV_HEADS * HEAD_DIM
WINDOW = 128
BLOCK = 128
T5_BUCKETS = 32
T5_MAX_DIST = 128
RMS_EPS = 1e-6
NEG_INF = -1e30
LOG2_E = math.log2(math.e)

K_WIDTH = ATT_WIDTH + WA_KV_WIDTH
ROW_GA, ROW_GB, ROW_QA, ROW_VA, ROW_ZA, ROW_QB, ROW_ZB, ROW_VB = (
    0, 1024, 2048, 2560, 3072, 3584, 4096, 4608)
T_WIDTH = 4736

NA_ROWS_PER_STEP = 4
NA_Q = NA_ROWS_PER_STEP * GRID_W
NA_KEY_SEGS = 3
PAIR = 2 * HEAD_DIM
PROJ_TM = 512
PROJ_ROW_CHUNK = 1184
MIX_TM = 512
VMEM_LIMIT = 52 * 1024 * 1024

_NT = (((1,), (1,)), ((), ()))
_TN = (((0,), (0,)), ((), ()))


def _t5_bucket(rel):
    half = T5_BUCKETS // 2
    exact = half // 2
    ret = jnp.where(rel > 0, half, 0)
    n = jnp.abs(rel)
    nf = jnp.maximum(n, 1).astype(F32)
    large = exact + (jnp.log(nf / exact) / math.log(T5_MAX_DIST / exact)
                     * (half - exact)).astype(jnp.int32)
    large = jnp.minimum(large, half - 1)
    return ret + jnp.where(n < exact, n, large)


def _t5_lookup(tab, rel):
    bucket = _t5_bucket(rel)
    tail = (1,) * bucket.ndim
    hit = bucket[None] == jnp.arange(T5_BUCKETS).reshape((T5_BUCKETS,) + tail)
    vals = jnp.transpose(tab).reshape((tab.shape[1], T5_BUCKETS) + tail)
    return jnp.sum(jnp.where(hit[None], vals, 0.0), axis=1)


def _toeplitz(v, rows, cols):
    length = rows + cols - 1
    assert v.shape[-1] == length
    lead = v.shape[:-1]
    flat = jnp.broadcast_to(v[..., None, :], lead + (rows, length)).reshape(lead + (rows * length,))
    flat = flat[..., rows - 1:rows - 1 + rows * (length - 1)]
    return flat.reshape(lead + (rows, length - 1))[..., :cols]


def _half_padded(q, upper):
    zeros = jnp.zeros_like(q)
    return jnp.concatenate([zeros, q] if upper else [q, zeros], axis=0)


def _norm_proj_body(x_ref, g_ref, wk_ref, wt_ref, ok_ref, ot_ref):
    x = x_ref[...]
    ms = jnp.mean(x * x, axis=-1, keepdims=True)
    u = ((x * lax.rsqrt(ms + RMS_EPS)) * g_ref[...]).astype(BF16)
    ok_ref[...] = jnp.dot(u, wk_ref[...], preferred_element_type=F32).astype(BF16)
    for r in range(0, T_WIDTH, PROJ_ROW_CHUNK):
        ot_ref[0, r:r + PROJ_ROW_CHUNK, :] = lax.dot_general(
            wt_ref[r:r + PROJ_ROW_CHUNK, :], u, _NT, preferred_element_type=F32).astype(BF16)


def _norm_proj(x, g, wk, wt, tm):
    b, n, _ = x.shape
    assert n % tm == 0
    return pl.pallas_call(
        _norm_proj_body,
        grid=(b, n // tm),
        in_specs=[pl.BlockSpec((None, tm, D_MODEL), lambda bb, i: (bb, i, 0)),
                  pl.BlockSpec((1, D_MODEL), lambda bb, i: (0, 0)),
                  pl.BlockSpec((D_MODEL, K_WIDTH), lambda bb, i: (0, 0)),
                  pl.BlockSpec((T_WIDTH, D_MODEL), lambda bb, i: (0, 0))],
        out_specs=[pl.BlockSpec((None, tm, K_WIDTH), lambda bb, i: (bb, i, 0)),
                   pl.BlockSpec((1, T_WIDTH, tm), lambda bb, i: (bb, 0, i))],
        out_shape=[jax.ShapeDtypeStruct((b, n, K_WIDTH), BF16),
                   jax.ShapeDtypeStruct((b, T_WIDTH, n), BF16)],
        compiler_params=pltpu.CompilerParams(
            dimension_semantics=("arbitrary", "arbitrary"), vmem_limit_bytes=VMEM_LIMIT),
        name="norm_proj",
    )(x, g, wk, wt)


def _na_bias_tables(rpb):
    n_t = NA_KEY_SEGS * NA_ROWS_PER_STEP
    lead = GRID_W - NA_WIN_COLS
    v = jnp.pad(rpb.astype(F32) * LOG2_E, ((0, 0), (0, 0), (lead, lead)))
    cols = _toeplitz(v, GRID_W, GRID_W)
    j = jnp.arange(GRID_W)[:, None]
    c = jnp.arange(GRID_W)[None, :]
    cs = jnp.clip(j - NA_WIN_COLS // 2, 0, GRID_W - NA_WIN_COLS)
    cols = jnp.where((c >= cs) & (c < cs + NA_WIN_COLS), cols, NEG_INF)
    cols_t = jnp.swapaxes(cols, -1, -2)
    masked = jnp.full((HEADS, GRID_W, GRID_W), NEG_INF, F32)
    row_ok = (
        lambda a, t: t >= NA_ROWS_PER_STEP,
        lambda a, t: 0 <= t - a < NA_WIN_ROWS,
        lambda a, t: t < NA_WIN_ROWS,
    )
    tabs = []
    for ok in row_ok:
        per_t = [jnp.stack([cols_t[:, t - a + NA_WIN_ROWS // 2 - 1] if ok(a, t) else masked
                            for a in range(NA_ROWS_PER_STEP)], axis=2)
                 for t in range(n_t)]
        tabs.append(jnp.stack(per_t, axis=1))
    return jnp.stack(tabs).reshape(3, HEADS, NA_KEY_SEGS * NA_Q, NA_Q)


NA_KEYS = NA_KEY_SEGS * NA_Q + N_META


def _na_body(q_ref, kp_ref, kc_ref, kn_ref, vp_ref, vc_ref, vn_ref, km_ref, vm_ref,
             bias_ref, o_ref, s_ref, m_ref):
    k_refs = (kp_ref, kc_ref, kn_ref)
    v_refs = (vp_ref, vc_ref, vn_ref)
    n_win = NA_KEY_SEGS * NA_Q

    @pl.when((pl.program_id(0) == 0) & (pl.program_id(1) == 0))
    def _():
        s_ref[...] = jnp.zeros_like(s_ref)
        m_ref[...] = jnp.zeros_like(m_ref)

    for h in range(HEADS):
        rows = slice(h * HEAD_DIM, (h + 1) * HEAD_DIM)
        m = m_ref[h, 0:1, :]
        p_meta = jnp.exp2(s_ref[h, n_win:NA_KEYS, :] - m)
        l = jnp.sum(p_meta, axis=0, keepdims=True)
        o = jnp.dot(vm_ref[0, rows, :], p_meta.astype(BF16), preferred_element_type=F32)
        for g, vr in enumerate(v_refs):
            p = jnp.exp2(s_ref[h, g * NA_Q:(g + 1) * NA_Q, :] - m)
            l = l + jnp.sum(p, axis=0, keepdims=True)
            o = o + jnp.dot(vr[0, rows, :], p.astype(BF16), preferred_element_type=F32)
        o_ref[0, rows, :] = (o / l).astype(BF16)

    for h in range(HEADS):
        rows = slice(h * HEAD_DIM, (h + 1) * HEAD_DIM)
        slab = slice((h // 2) * PAIR, (h // 2 + 1) * PAIR)
        q = _half_padded(q_ref[0, rows, :], h % 2)
        s_meta = jnp.dot(km_ref[:, slab], q, preferred_element_type=F32)
        s_ref[h, n_win:NA_KEYS, :] = s_meta
        m = jnp.max(s_meta, axis=0, keepdims=True)
        for g, kr in enumerate(k_refs):
            sg = (jnp.dot(kr[0, :, slab], q, preferred_element_type=F32)
                  + bias_ref[0, h, g * NA_Q:(g + 1) * NA_Q, :])
            s_ref[h, g * NA_Q:(g + 1) * NA_Q, :] = sg
            m = jnp.maximum(m, jnp.max(sg, axis=0, keepdims=True))
        m_ref[h, 0:1, :] = m


def _na_attend(pk, pt, meta_pk, meta_pt, bias_tabs):
    b, n, _ = pk.shape
    assert n % NA_Q == 0
    nblk = n // NA_Q
    assert nblk >= 3
    rq, rv = ROW_QA // ATT_WIDTH, ROW_VA // ATT_WIDTH

    def scored(i):
        return jnp.minimum(i, nblk - 1)

    def finished(i):
        return jnp.maximum(i - 1, 0)

    def shifted(blk, shift):
        return jnp.clip(blk + shift, 0, nblk - 1)

    def kseg(shift):
        return pl.BlockSpec((1, NA_Q, ATT_WIDTH),
                            lambda bb, i: (bb, shifted(scored(i), shift), 0))

    def vseg(shift):
        return pl.BlockSpec((1, ATT_WIDTH, NA_Q),
                            lambda bb, i: (bb, rv, shifted(finished(i), shift)))

    def step_kind(bb, i):
        blk = scored(i)
        return (jnp.where(blk == 0, 0, jnp.where(blk == nblk - 1, 2, 1)), 0, 0, 0)

    return pl.pallas_call(
        _na_body,
        grid=(b, nblk + 1),
        in_specs=[pl.BlockSpec((1, ATT_WIDTH, NA_Q), lambda bb, i: (bb, rq, scored(i))),
                  kseg(-1), kseg(0), kseg(1),
                  vseg(-1), vseg(0), vseg(1),
                  pl.BlockSpec((N_META, ATT_WIDTH), lambda bb, i: (0, 0)),
                  pl.BlockSpec((1, ATT_WIDTH, N_META), lambda bb, i: (0, rv, 0)),
                  pl.BlockSpec((1, HEADS, NA_KEY_SEGS * NA_Q, NA_Q), step_kind)],
        out_specs=pl.BlockSpec((1, ATT_WIDTH, NA_Q), lambda bb, i: (bb, 0, finished(i))),
        out_shape=jax.ShapeDtypeStruct((b, ATT_WIDTH, n), BF16),
        scratch_shapes=[pltpu.VMEM((HEADS, NA_KEYS, NA_Q), F32),
                        pltpu.VMEM((HEADS, 8, NA_Q), F32)],
        compiler_params=pltpu.CompilerParams(
            dimension_semantics=("arbitrary", "arbitrary"), vmem_limit_bytes=VMEM_LIMIT),
        name="na_attend",
    )(pt, pk, pk, pk, pt, pt, pt, meta_pk, meta_pt, bias_tabs)


def _wa_bias_tables(t5_bias, sink, nb):
    tab = t5_bias.astype(F32) * LOG2_E
    qq = jnp.arange(BLOCK)
    kk = jnp.arange(3 * BLOCK)
    rel = kk[None, :] - BLOCK - qq[:, None]
    dist = jnp.arange(-(2 * BLOCK - 1), 2 * BLOCK)
    band = _toeplitz(_t5_lookup(tab, dist), BLOCK, 3 * BLOCK)
    in_window = jnp.abs(rel) <= WINDOW
    seg = (kk // BLOCK)[None, :]
    kinds = (in_window & (seg != 0), in_window, in_window & (seg != 2))
    band = jnp.stack([jnp.where(ok[None], band, NEG_INF) for ok in kinds])
    band = band.reshape(3, WA_KV_HEADS, WA_GROUP, BLOCK, 3 * BLOCK)
    band = jnp.transpose(band, (0, 1, 4, 2, 3)).reshape(3, WA_KV_HEADS, 3 * BLOCK, WA_GROUP * BLOCK)
    q_pos = N_META + jnp.arange(nb)[:, None] * BLOCK + qq[None, :]
    meta_rel = jnp.arange(N_META)[None, None, :] - q_pos[:, :, None]
    meta = _t5_lookup(tab, meta_rel).reshape(WA_KV_HEADS, WA_GROUP, nb, BLOCK, N_META)
    meta = jnp.transpose(meta, (2, 0, 4, 1, 3)).reshape(nb, WA_KV_HEADS, N_META, WA_GROUP * BLOCK)
    sink_row = jnp.broadcast_to((sink.astype(F32) * LOG2_E).reshape(WA_KV_HEADS, WA_GROUP, 1),
                                (WA_KV_HEADS, WA_GROUP, BLOCK)).reshape(WA_KV_HEADS, 1, WA_GROUP * BLOCK)
    return band, meta, sink_row


WA_KEYS = 3 * BLOCK + N_META


def _wa_body(q_ref, kp_ref, kc_ref, kn_ref, vp_ref, vc_ref, vn_ref, km_ref, vm_ref,
             band_ref, mbias_ref, sink_ref, o_ref, s_ref, m_ref):
    k_refs = (kp_ref, kc_ref, kn_ref)
    v_refs = (vp_ref, vc_ref, vn_ref)
    n_band = 3 * BLOCK

    @pl.when((pl.program_id(0) == 0) & (pl.program_id(1) == 0))
    def _():
        s_ref[...] = jnp.zeros_like(s_ref)
        m_ref[...] = jnp.zeros_like(m_ref)

    for kv in range(WA_KV_HEADS):
        vrows = slice(kv * HEAD_DIM, (kv + 1) * HEAD_DIM)
        m = m_ref[kv, 0:1, :]
        p_meta = jnp.exp2(s_ref[kv, n_band:WA_KEYS, :] - m)
        l = jnp.sum(p_meta, axis=0, keepdims=True) + jnp.exp2(sink_ref[kv] - m)
        o = jnp.dot(vm_ref[0, vrows, :], p_meta.astype(BF16), preferred_element_type=F32)
        for g, vr in enumerate(v_refs):
            p = jnp.exp2(s_ref[kv, g * BLOCK:(g + 1) * BLOCK, :] - m)
            l = l + jnp.sum(p, axis=0, keepdims=True)
            o = o + jnp.dot(vr[0, vrows, :], p.astype(BF16), preferred_element_type=F32)
        o = (o / l).astype(BF16)
        for g in range(WA_GROUP):
            h = kv * WA_GROUP + g
            o_ref[0, h * HEAD_DIM:(h + 1) * HEAD_DIM, :] = o[:, g * BLOCK:(g + 1) * BLOCK]

    for kv in range(WA_KV_HEADS):
        q4 = jnp.concatenate(
            [q_ref[0, (kv * WA_GROUP + g) * HEAD_DIM:(kv * WA_GROUP + g + 1) * HEAD_DIM, :]
             for g in range(WA_GROUP)], axis=1)
        q4 = _half_padded(q4, kv)
        s_meta = jnp.dot(km_ref[...], q4, preferred_element_type=F32) + mbias_ref[0, kv]
        s_ref[kv, n_band:WA_KEYS, :] = s_meta
        m = jnp.maximum(jnp.max(s_meta, axis=0, keepdims=True), sink_ref[kv])
        for g, kr in enumerate(k_refs):
            sg = (jnp.dot(kr[0], q4, preferred_element_type=F32)
                  + band_ref[0, kv, g * BLOCK:(g + 1) * BLOCK, :])
            s_ref[kv, g * BLOCK:(g + 1) * BLOCK, :] = sg
            m = jnp.maximum(m, jnp.max(sg, axis=0, keepdims=True))
        m_ref[kv, 0:1, :] = m


def _wa_attend(pk, pt, meta_pk, meta_pt, band, meta_bias, sink_row):
    b, n, _ = pk.shape
    assert n % BLOCK == 0
    nb = n // BLOCK
    assert nb >= 2
    rq, rv = ROW_QB // ATT_WIDTH, ROW_VB // WA_KV_WIDTH
    ck = ATT_WIDTH // WA_KV_WIDTH

    def scored(i):
        return jnp.minimum(i, nb - 1)

    def finished(i):
        return jnp.maximum(i - 1, 0)

    def shifted(blk, shift):
        return jnp.clip(blk + shift, 0, nb - 1)

    def kseg(shift):
        return pl.BlockSpec((1, BLOCK, WA_KV_WIDTH),
                            lambda bb, i: (bb, shifted(scored(i), shift), ck))

    def vseg(shift):
        return pl.BlockSpec((1, WA_KV_WIDTH, BLOCK),
                            lambda bb, i: (bb, rv, shifted(finished(i), shift)))

    def block_kind(bb, i):
        blk = scored(i)
        return (jnp.where(blk == 0, 0, jnp.where(blk == nb - 1, 2, 1)), 0, 0, 0)

    lanes = WA_GROUP * BLOCK
    return pl.pallas_call(
        _wa_body,
        grid=(b, nb + 1),
        in_specs=[pl.BlockSpec((1, ATT_WIDTH, BLOCK), lambda bb, i: (bb, rq, scored(i))),
                  kseg(-1), kseg(0), kseg(1),
                  vseg(-1), vseg(0), vseg(1),
                  pl.BlockSpec((N_META, WA_KV_WIDTH), lambda bb, i: (0, ck)),
                  pl.BlockSpec((1, WA_KV_WIDTH, N_META), lambda bb, i: (0, rv, 0)),
                  pl.BlockSpec((1, WA_KV_HEADS, 3 * BLOCK, lanes), block_kind),
                  pl.BlockSpec((1, WA_KV_HEADS, N_META, lanes),
                               lambda bb, i: (scored(i), 0, 0, 0)),
                  pl.BlockSpec((WA_KV_HEADS, 1, lanes), lambda bb, i: (0, 0, 0))],
        out_specs=pl.BlockSpec((1, ATT_WIDTH, BLOCK), lambda bb, i: (bb, 0, finished(i))),
        out_shape=jax.ShapeDtypeStruct((b, ATT_WIDTH, n), BF16),
        scratch_shapes=[pltpu.VMEM((WA_KV_HEADS, WA_KEYS, lanes), F32),
                        pltpu.VMEM((WA_KV_HEADS, 8, lanes), F32)],
        compiler_params=pltpu.CompilerParams(
            dimension_semantics=("arbitrary", "arbitrary"), vmem_limit_bytes=VMEM_LIMIT),
        name="wa_attend",
    )(pt, pk, pk, pk, pt, pt, pt, meta_pk, meta_pt, band, meta_bias, sink_row)


def _mix_out_body(x_ref, oa_ref, ob_ref, za_ref, zb_ref, ga_ref, gb_ref,
                  wpa_ref, wpb_ref, wout_ref, fg_ref, y_ref):
    ta = (oa_ref[0].astype(F32) * jax.nn.silu(za_ref[0].astype(F32))).astype(BF16)
    tb = (ob_ref[0].astype(F32) * jax.nn.silu(zb_ref[0].astype(F32))).astype(BF16)
    ya = jnp.dot(wpa_ref[...], ta, preferred_element_type=F32)
    yb = jnp.dot(wpb_ref[...], tb, preferred_element_type=F32)
    merged = (jax.nn.sigmoid(ga_ref[0].astype(F32)) * ya
              + jax.nn.sigmoid(gb_ref[0].astype(F32)) * yb).astype(BF16)
    h = x_ref[0] + lax.dot_general(merged, wout_ref[...], _TN, preferred_element_type=F32)
    ms = jnp.mean(h * h, axis=-1, keepdims=True)
    y_ref[0] = (h * lax.rsqrt(ms + RMS_EPS)) * fg_ref[...]


def _mix_out(x, oa, ob, pt, wpa_t, wpb_t, wout, fg, tm):
    b, n, _ = x.shape
    assert n % tm == 0

    def feat(rows, row0):
        return pl.BlockSpec((1, rows, tm), lambda bb, i: (bb, row0 // rows, i))

    def whole(shape):
        return pl.BlockSpec(shape, lambda bb, i: (0, 0))

    tok = pl.BlockSpec((1, tm, D_MODEL), lambda bb, i: (bb, i, 0))
    return pl.pallas_call(
        _mix_out_body,
        grid=(b, n // tm),
        in_specs=[tok, feat(ATT_WIDTH, 0), feat(ATT_WIDTH, 0),
                  feat(ATT_WIDTH, ROW_ZA), feat(ATT_WIDTH, ROW_ZB),
                  feat(D_MODEL, ROW_GA), feat(D_MODEL, ROW_GB),
                  whole((D_MODEL, ATT_WIDTH)), whole((D_MODEL, ATT_WIDTH)),
                  whole((D_MODEL, D_MODEL)), whole((1, D_MODEL))],
        out_specs=tok,
        out_shape=jax.ShapeDtypeStruct((b, n, D_MODEL), F32),
        compiler_params=pltpu.CompilerParams(
            dimension_semantics=("arbitrary", "arbitrary"), vmem_limit_bytes=VMEM_LIMIT),
        name="mix_out",
    )(x, oa, ob, pt, pt, pt, pt, wpa_t, wpb_t, wout, fg)


def _encode(x, meta_pk, meta_pt, params):
    n = x.shape[1]
    pk, pt = _norm_proj(x, params["norm_g"], params["w_k"], params["w_t"], PROJ_TM)
    oa = _na_attend(pk, pt, meta_pk, meta_pt, params["na_bias"])
    band, meta_bias, sink_row = _wa_bias_tables(params["t5_bias"], params["sink"], n // BLOCK)
    ob = _wa_attend(pk, pt, meta_pk, meta_pt, band, meta_bias, sink_row)
    return _mix_out(x, oa, ob, pt, params["w_proj_a_t"], params["w_proj_b_t"], params["w_out"],
                    params["final_g"], MIX_TM)


def kernel(x_prompt, x_sample, meta_tokens, norm_g, w_in, na_rpb, sink_logit, w_proj_a, w_proj_b,
           w_out, t5_bias, final_g):
    assert norm_g.shape[0] == 1, "one layer"
    w = w_in[0]
    scale = HEAD_DIM ** -0.5 * LOG2_E
    q_a, k_a, v_a, z_a = (w[:, i * 512:(i + 1) * 512] for i in range(4))
    q_b, k_b, v_b, z_b = w[:, 2048:2560], w[:, 2560:2688], w[:, 2688:2816], w[:, 2816:3328]
    g_a, g_b = w[:, 3328:4352], w[:, 4352:5376]
    w_t = jnp.concatenate([g_a, g_b, q_a * scale, v_a, z_a, q_b * scale, z_b, v_b], axis=1)
    params = {
        "norm_g": norm_g[0].reshape(1, D_MODEL).astype(F32),
        "w_k": jnp.concatenate([k_a, k_b], axis=1).astype(BF16),
        "w_t": jnp.transpose(w_t).astype(BF16),
        "na_bias": _na_bias_tables(na_rpb[0]),
        "t5_bias": t5_bias,
        "sink": sink_logit[0],
        "w_proj_a_t": jnp.transpose(w_proj_a[0]).astype(BF16),
        "w_proj_b_t": jnp.transpose(w_proj_b[0]).astype(BF16),
        "w_out": w_out[0].astype(BF16),
        "final_g": final_g.reshape(1, D_MODEL).astype(F32),
    }
    meta_pk, meta_pt = _norm_proj(meta_tokens.astype(F32)[None], params["norm_g"],
                                  params["w_k"], params["w_t"], N_META)
    return (_encode(x_prompt, meta_pk[0], meta_pt, params),
            _encode(x_sample, meta_pk[0], meta_pt, params))
```

```python
import math

import jax
import jax.numpy as jnp
from jax import lax
from jax.experimental import pallas as pl
from jax.experimental.pallas import tpu as pltpu

F32 = jnp.float32
BF16 = jnp.bfloat16

D_MODEL = 1024
N_META = 16
GRID_W = 64
HEADS = 8
HEAD_DIM = 64
ATT_WIDTH = HEADS * HEAD_DIM
NA_WIN_ROWS = 8
NA_WIN_COLS = 16
WA_KV_HEADS = 2
WA_GROUP = HEADS // WA_KV_HEADS
WA_KV_WIDTH = WA_KV_HEADS * HEAD_DIM
WINDOW = 128
BLOCK = 128
T5_BUCKETS = 32
T5_MAX_DIST = 128
RMS_EPS = 1e-6
NEG_INF = -1e30
LOG2_E = math.log2(math.e)

K_WIDTH = ATT_WIDTH + WA_KV_WIDTH
ROW_GA, ROW_GB, ROW_QA, ROW_VA, ROW_ZA, ROW_QB, ROW_ZB, ROW_VB = (
    0, 1024, 2048, 2560, 3072, 3584, 4096, 4608)
T_WIDTH = 4736

NA_ROWS_PER_STEP = 4
NA_Q = NA_ROWS_PER_STEP * GRID_W
NA_KEY_SEGS = 3
PAIR = 2 * HEAD_DIM
PROJ_TM = 512
PROJ_ROW_CHUNK = 1184
MIX_TM = 512
VMEM_LIMIT = 52 * 1024 * 1024

_NT = (((1,), (1,)), ((), ()))
_TN = (((0,), (0,)), ((), ()))


def _t5_bucket(rel):
    half = T5_BUCKETS // 2
    exact = half // 2
    ret = jnp.where(rel > 0, half, 0)
    n = jnp.abs(rel)
    nf = jnp.maximum(n, 1).astype(F32)
    large = exact + (jnp.log(nf / exact) / math.log(T5_MAX_DIST / exact)
                     * (half - exact)).astype(jnp.int32)
    large = jnp.minimum(large, half - 1)
    return ret + jnp.where(n < exact, n, large)


def _t5_lookup(tab, rel):
    bucket = _t5_bucket(rel)
    tail = (1,) * bucket.ndim
    hit = bucket[None] == jnp.arange(T5_BUCKETS).reshape((T5_BUCKETS,) + tail)
    vals = jnp.transpose(tab).reshape((tab.shape[1], T5_BUCKETS) + tail)
    return jnp.sum(jnp.where(hit[None], vals, 0.0), axis=1)


def _toeplitz(v, rows, cols):
    length = rows + cols - 1
    assert v.shape[-1] == length
    lead = v.shape[:-1]
    flat = jnp.broadcast_to(v[..., None, :], lead + (rows, length)).reshape(lead + (rows * length,))
    flat = flat[..., rows - 1:rows - 1 + rows * (length - 1)]
    return flat.reshape(lead + (rows, length - 1))[..., :cols]


ONES_ROWS = 16


def _with_ones_rows(v):
    return jnp.concatenate([v, jnp.ones((ONES_ROWS, v.shape[1]), v.dtype)], axis=0)


def _half_padded(q, upper):
    zeros = jnp.zeros_like(q)
    return jnp.concatenate([zeros, q] if upper else [q, zeros], axis=0)


def _norm_proj_body(x_ref, g_ref, wk_ref, wt_ref, ok_ref, ot_ref):
    x = x_ref[...]
    ms = jnp.mean(x * x, axis=-1, keepdims=True)
    u = ((x * lax.rsqrt(ms + RMS_EPS)) * g_ref[...]).astype(BF16)
    ok_ref[...] = jnp.dot(u, wk_ref[...], preferred_element_type=F32).astype(BF16)
    for r in range(0, T_WIDTH, PROJ_ROW_CHUNK):
        ot_ref[0, r:r + PROJ_ROW_CHUNK, :] = lax.dot_general(
            wt_ref[r:r + PROJ_ROW_CHUNK, :], u, _NT, preferred_element_type=F32).astype(BF16)


def _norm_proj(x, g, wk, wt, tm):
    b, n, _ = x.shape
    assert n % tm == 0
    return pl.pallas_call(
        _norm_proj_body,
        grid=(b, n // tm),
        in_specs=[pl.BlockSpec((None, tm, D_MODEL), lambda bb, i: (bb, i, 0)),
                  pl.BlockSpec((1, D_MODEL), lambda bb, i: (0, 0)),
                  pl.BlockSpec((D_MODEL, K_WIDTH), lambda bb, i: (0, 0)),
                  pl.BlockSpec((T_WIDTH, D_MODEL), lambda bb, i: (0, 0))],
        out_specs=[pl.BlockSpec((None, tm, K_WIDTH), lambda bb, i: (bb, i, 0)),
                   pl.BlockSpec((1, T_WIDTH, tm), lambda bb, i: (bb, 0, i))],
        out_shape=[jax.ShapeDtypeStruct((b, n, K_WIDTH), BF16),
                   jax.ShapeDtypeStruct((b, T_WIDTH, n), BF16)],
        compiler_params=pltpu.CompilerParams(
            dimension_semantics=("arbitrary", "arbitrary"), vmem_limit_bytes=VMEM_LIMIT),
        name="norm_proj",
    )(x, g, wk, wt)


def _na_bias_tables(rpb):
    n_t = NA_KEY_SEGS * NA_ROWS_PER_STEP
    lead = GRID_W - NA_WIN_COLS
    v = jnp.pad(rpb.astype(F32) * LOG2_E, ((0, 0), (0, 0), (lead, lead)))
    cols = _toeplitz(v, GRID_W, GRID_W)
    j = jnp.arange(GRID_W)[:, None]
    c = jnp.arange(GRID_W)[None, :]
    cs = jnp.clip(j - NA_WIN_COLS // 2, 0, GRID_W - NA_WIN_COLS)
    cols = jnp.where((c >= cs) & (c < cs + NA_WIN_COLS), cols, NEG_INF)
    assert NA_WIN_ROWS // 2 - 1 == NA_ROWS_PER_STEP - 1 and cols.shape[1] == NA_ROWS_PER_STEP + n_t - 1
    full = _toeplitz(jnp.transpose(cols, (0, 3, 2, 1)), NA_ROWS_PER_STEP, n_t)
    full = jnp.transpose(full, (0, 4, 1, 3, 2))
    a = jnp.arange(NA_ROWS_PER_STEP)[None, :]
    t = jnp.arange(n_t)[:, None]
    row_ok = (
        (t >= NA_ROWS_PER_STEP) & (a >= 0),
        (t - a >= 0) & (t - a < NA_WIN_ROWS),
        (t < NA_WIN_ROWS) & (a >= 0),
    )
    tabs = [jnp.where(ok[None, :, None, :, None], full, NEG_INF) for ok in row_ok]
    return jnp.stack(tabs).reshape(3, HEADS, NA_KEY_SEGS * NA_Q, NA_Q)


NA_KEYS = NA_KEY_SEGS * NA_Q + N_META


def _na_body(q_ref, kp_ref, kc_ref, kn_ref, vp_ref, vc_ref, vn_ref, km_ref, vm_ref,
             bias_ref, o_ref, s0_ref, m0_ref, s1_ref, m1_ref):
    k_refs = (kp_ref, kc_ref, kn_ref)
    v_refs = (vp_ref, vc_ref, vn_ref)
    n_win = NA_KEY_SEGS * NA_Q
    step = pl.program_id(0) * pl.num_programs(1) + pl.program_id(1)

    @pl.when(step == 0)
    def _():
        s1_ref[...] = jnp.zeros_like(s1_ref)
        m1_ref[...] = jnp.zeros_like(m1_ref)

    def stages(s_prev, m_prev, s_next, m_next):
        for h in range(HEADS):
            rows = slice(h * HEAD_DIM, (h + 1) * HEAD_DIM)
            m = m_prev[h, 0:1, :]
            p_meta = jnp.exp2(s_prev[h, n_win:NA_KEYS, :] - m)
            o = jnp.dot(_with_ones_rows(vm_ref[0, rows, :]), p_meta.astype(BF16),
                        preferred_element_type=F32)
            for g, vr in enumerate(v_refs):
                p = jnp.exp2(s_prev[h, g * NA_Q:(g + 1) * NA_Q, :] - m)
                o = o + jnp.dot(_with_ones_rows(vr[0, rows, :]), p.astype(BF16),
                                preferred_element_type=F32)
            o_ref[0, rows, :] = (o[:HEAD_DIM] / o[HEAD_DIM:HEAD_DIM + 1]).astype(BF16)
            slab = slice((h // 2) * PAIR, (h // 2 + 1) * PAIR)
            q = _half_padded(q_ref[0, rows, :], h % 2)
            s_meta = jnp.dot(km_ref[:, slab], q, preferred_element_type=F32)
            s_next[h, n_win:NA_KEYS, :] = s_meta
            m = jnp.max(s_meta, axis=0, keepdims=True)
            for g, kr in enumerate(k_refs):
                sg = (jnp.dot(kr[0, :, slab], q, preferred_element_type=F32)
                      + bias_ref[0, h, g * NA_Q:(g + 1) * NA_Q, :])
                s_next[h, g * NA_Q:(g + 1) * NA_Q, :] = sg
                m = jnp.maximum(m, jnp.max(sg, axis=0, keepdims=True))
            m_next[h, 0:1, :] = m

    @pl.when(step % 2 == 0)
    def _():
        stages(s1_ref, m1_ref, s0_ref, m0_ref)

    @pl.when(step % 2 == 1)
    def _():
        stages(s0_ref, m0_ref, s1_ref, m1_ref)


def _na_attend(pk, pt, meta_pk, meta_pt, bias_tabs):
    b, n, _ = pk.shape
    assert n % NA_Q == 0
    nblk = n // NA_Q
    assert nblk >= 3
    rq, rv = ROW_QA // ATT_WIDTH, ROW_VA // ATT_WIDTH

    def scored(i):
        return jnp.minimum(i, nblk - 1)

    def finished(i):
        return jnp.maximum(i - 1, 0)

    def shifted(blk, shift):
        return jnp.clip(blk + shift, 0, nblk - 1)

    def kseg(shift):
        return pl.BlockSpec((1, NA_Q, ATT_WIDTH),
                            lambda bb, i: (bb, shifted(scored(i), shift), 0))

    def vseg(shift):
        return pl.BlockSpec((1, ATT_WIDTH, NA_Q),
                            lambda bb, i: (bb, rv, shifted(finished(i), shift)))

    def step_kind(bb, i):
        blk = scored(i)
        return (jnp.where(blk == 0, 0, jnp.where(blk == nblk - 1, 2, 1)), 0, 0, 0)

    return pl.pallas_call(
        _na_body,
        grid=(b, nblk + 1),
        in_specs=[pl.BlockSpec((1, ATT_WIDTH, NA_Q), lambda bb, i: (bb, rq, scored(i))),
                  kseg(-1), kseg(0), kseg(1),
                  vseg(-1), vseg(0), vseg(1),
                  pl.BlockSpec((N_META, ATT_WIDTH), lambda bb, i: (0, 0)),
                  pl.BlockSpec((1, ATT_WIDTH, N_META), lambda bb, i: (0, rv, 0)),
                  pl.BlockSpec((1, HEADS, NA_KEY_SEGS * NA_Q, NA_Q), step_kind)],
        out_specs=pl.BlockSpec((1, ATT_WIDTH, NA_Q), lambda bb, i: (bb, 0, finished(i))),
        out_shape=jax.ShapeDtypeStruct((b, ATT_WIDTH, n), BF16),
        scratch_shapes=[pltpu.VMEM((HEADS, NA_KEYS, NA_Q), F32),
                        pltpu.VMEM((HEADS, 8, NA_Q), F32)] * 2,
        compiler_params=pltpu.CompilerParams(
            dimension_semantics=("arbitrary", "arbitrary"), vmem_limit_bytes=VMEM_LIMIT),
        name="na_attend",
    )(pt, pk, pk, pk, pt, pt, pt, meta_pk, meta_pt, bias_tabs)


def _wa_bias_tables(t5_bias, sink, nb):
    tab = t5_bias.astype(F32) * LOG2_E
    qq = jnp.arange(BLOCK)
    kk = jnp.arange(3 * BLOCK)
    rel = kk[None, :] - BLOCK - qq[:, None]
    dist = jnp.arange(-(2 * BLOCK - 1), 2 * BLOCK)
    band = _toeplitz(_t5_lookup(tab, dist), BLOCK, 3 * BLOCK)
    in_window = jnp.abs(rel) <= WINDOW
    seg = (kk // BLOCK)[None, :]
    kinds = (in_window & (seg != 0), in_window, in_window & (seg != 2))
    band = jnp.stack([jnp.where(ok[None], band, NEG_INF) for ok in kinds])
    band = band.reshape(3, WA_KV_HEADS, WA_GROUP, BLOCK, 3 * BLOCK)
    band = jnp.transpose(band, (0, 1, 4, 2, 3)).reshape(3, WA_KV_HEADS, 3 * BLOCK, WA_GROUP * BLOCK)
    n = nb * BLOCK
    per_dist = _t5_lookup(tab, -jnp.arange(n + BLOCK + N_META))
    main = per_dist[:, 1:1 + n].reshape(HEADS, nb, BLOCK)
    tail = per_dist[:, 1 + BLOCK:1 + BLOCK + n].reshape(HEADS, nb, BLOCK)[..., :N_META - 1]
    meta = _toeplitz(jnp.concatenate([main, tail], axis=-1), N_META, BLOCK)
    meta = meta.reshape(WA_KV_HEADS, WA_GROUP, nb, N_META, BLOCK)
    meta = jnp.transpose(meta, (2, 0, 3, 1, 4)).reshape(nb, WA_KV_HEADS, N_META, WA_GROUP * BLOCK)
    sink_row = jnp.broadcast_to((sink.astype(F32) * LOG2_E).reshape(WA_KV_HEADS, WA_GROUP, 1),
                                (WA_KV_HEADS, WA_GROUP, BLOCK)).reshape(WA_KV_HEADS, 1, WA_GROUP * BLOCK)
    return band, meta, sink_row


WA_KEYS = 3 * BLOCK + N_META


def _wa_body(q_ref, kp_ref, kc_ref, kn_ref, vp_ref, vc_ref, vn_ref, km_ref, vm_ref,
             band_ref, mbias_ref, sink_ref, o_ref, s0_ref, m0_ref, s1_ref, m1_ref):
    k_refs = (kp_ref, kc_ref, kn_ref)
    v_refs = (vp_ref, vc_ref, vn_ref)
    n_band = 3 * BLOCK
    step = pl.program_id(0) * pl.num_programs(1) + pl.program_id(1)

    @pl.when(step == 0)
    def _():
        s1_ref[...] = jnp.zeros_like(s1_ref)
        m1_ref[...] = jnp.zeros_like(m1_ref)

    def stages(s_prev, m_prev, s_next, m_next):
        for kv in range(WA_KV_HEADS):
            vrows = slice(kv * HEAD_DIM, (kv + 1) * HEAD_DIM)
            m = m_prev[kv, 0:1, :]
            p_meta = jnp.exp2(s_prev[kv, n_band:WA_KEYS, :] - m)
            o = jnp.dot(_with_ones_rows(vm_ref[0, vrows, :]), p_meta.astype(BF16),
                        preferred_element_type=F32)
            for g, vr in enumerate(v_refs):
                p = jnp.exp2(s_prev[kv, g * BLOCK:(g + 1) * BLOCK, :] - m)
                o = o + jnp.dot(_with_ones_rows(vr[0, vrows, :]), p.astype(BF16),
                                preferred_element_type=F32)
            l = o[HEAD_DIM:HEAD_DIM + 1] + jnp.exp2(sink_ref[kv] - m)
            o = (o[:HEAD_DIM] / l).astype(BF16)
            for g in range(WA_GROUP):
                h = kv * WA_GROUP + g
                o_ref[0, h * HEAD_DIM:(h + 1) * HEAD_DIM, :] = o[:, g * BLOCK:(g + 1) * BLOCK]
            q4 = jnp.concatenate(
                [q_ref[0, (kv * WA_GROUP + g) * HEAD_DIM:(kv * WA_GROUP + g + 1) * HEAD_DIM, :]
                 for g in range(WA_GROUP)], axis=1)
            q4 = _half_padded(q4, kv)
            s_meta = jnp.dot(km_ref[...], q4, preferred_element_type=F32) + mbias_ref[0, kv]
            s_next[kv, n_band:WA_KEYS, :] = s_meta
            m = jnp.maximum(jnp.max(s_meta, axis=0, keepdims=True), sink_ref[kv])
            for g, kr in enumerate(k_refs):
                sg = (jnp.dot(kr[0], q4, preferred_element_type=F32)
                      + band_ref[0, kv, g * BLOCK:(g + 1) * BLOCK, :])
                s_next[kv, g * BLOCK:(g + 1) * BLOCK, :] = sg
                m = jnp.maximum(m, jnp.max(sg, axis=0, keepdims=True))
            m_next[kv, 0:1, :] = m

    @pl.when(step % 2 == 0)
    def _():
        stages(s1_ref, m1_ref, s0_ref, m0_ref)

    @pl.when(step % 2 == 1)
    def _():
        stages(s0_ref, m0_ref, s1_ref, m1_ref)


def _wa_attend(pk, pt, meta_pk, meta_pt, band, meta_bias, sink_row):
    b, n, _ = pk.shape
    assert n % BLOCK == 0
    nb = n // BLOCK
    assert nb >= 2
    rq, rv = ROW_QB // ATT_WIDTH, ROW_VB // WA_KV_WIDTH
    ck = ATT_WIDTH // WA_KV_WIDTH

    def scored(i):
        return jnp.minimum(i, nb - 1)

    def finished(i):
        return jnp.maximum(i - 1, 0)

    def shifted(blk, shift):
        return jnp.clip(blk + shift, 0, nb - 1)

    def kseg(shift):
        return pl.BlockSpec((1, BLOCK, WA_KV_WIDTH),
                            lambda bb, i: (bb, shifted(scored(i), shift), ck))

    def vseg(shift):
        return pl.BlockSpec((1, WA_KV_WIDTH, BLOCK),
                            lambda bb, i: (bb, rv, shifted(finished(i), shift)))

    def block_kind(bb, i):
        blk = scored(i)
        return (jnp.where(blk == 0, 0, jnp.where(blk == nb - 1, 2, 1)), 0, 0, 0)

    lanes = WA_GROUP * BLOCK
    return pl.pallas_call(
        _wa_body,
        grid=(b, nb + 1),
        in_specs=[pl.BlockSpec((1, ATT_WIDTH, BLOCK), lambda bb, i: (bb, rq, scored(i))),
                  kseg(-1), kseg(0), kseg(1),
                  vseg(-1), vseg(0), vseg(1),
                  pl.BlockSpec((N_META, WA_KV_WIDTH), lambda bb, i: (0, ck)),
                  pl.BlockSpec((1, WA_KV_WIDTH, N_META), lambda bb, i: (0, rv, 0)),
                  pl.BlockSpec((1, WA_KV_HEADS, 3 * BLOCK, lanes), block_kind),
                  pl.BlockSpec((1, WA_KV_HEADS, N_META, lanes),
                               lambda bb, i: (scored(i), 0, 0, 0)),
                  pl.BlockSpec((WA_KV_HEADS, 1, lanes), lambda bb, i: (0, 0, 0))],
        out_specs=pl.BlockSpec((1, ATT_WIDTH, BLOCK), lambda bb, i: (bb, 0, finished(i))),
        out_shape=jax.ShapeDtypeStruct((b, ATT_WIDTH, n), BF16),
        scratch_shapes=[pltpu.VMEM((WA_KV_HEADS, WA_KEYS, lanes), F32),
                        pltpu.VMEM((WA_KV_HEADS, 8, lanes), F32)] * 2,
        compiler_params=pltpu.CompilerParams(
            dimension_semantics=("arbitrary", "arbitrary"), vmem_limit_bytes=VMEM_LIMIT),
        name="wa_attend",
    )(pt, pk, pk, pk, pt, pt, pt, meta_pk, meta_pt, band, meta_bias, sink_row)


def _sigmoid(x):
    return 0.5 * jnp.tanh(0.5 * x) + 0.5


def _mix_out_body(x_ref, oa_ref, ob_ref, za_ref, zb_ref, ga_ref, gb_ref,
                  wpa_ref, wpb_ref, wout_ref, fg_ref, y_ref):
    za = za_ref[0].astype(F32)
    zb = zb_ref[0].astype(F32)
    ta = (oa_ref[0].astype(F32) * (za * _sigmoid(za))).astype(BF16)
    tb = (ob_ref[0].astype(F32) * (zb * _sigmoid(zb))).astype(BF16)
    ya = jnp.dot(wpa_ref[...], ta, preferred_element_type=F32)
    yb = jnp.dot(wpb_ref[...], tb, preferred_element_type=F32)
    merged = (_sigmoid(ga_ref[0].astype(F32)) * ya
              + _sigmoid(gb_ref[0].astype(F32)) * yb).astype(BF16)
    h = x_ref[0] + lax.dot_general(merged, wout_ref[...], _TN, preferred_element_type=F32)
    ms = jnp.mean(h * h, axis=-1, keepdims=True)
    y_ref[0] = (h * lax.rsqrt(ms + RMS_EPS)) * fg_ref[...]


def _mix_out(x, oa, ob, pt, wpa_t, wpb_t, wout, fg, tm):
    b, n, _ = x.shape
    assert n % tm == 0

    def feat(rows, row0):
        return pl.BlockSpec((1, rows, tm), lambda bb, i: (bb, row0 // rows, i))

    def whole(shape):
        return pl.BlockSpec(shape, lambda bb, i: (0, 0))

    tok = pl.BlockSpec((1, tm, D_MODEL), lambda bb, i: (bb, i, 0))
    return pl.pallas_call(
        _mix_out_body,
        grid=(b, n // tm),
        in_specs=[tok, feat(ATT_WIDTH, 0), feat(ATT_WIDTH, 0),
                  feat(ATT_WIDTH, ROW_ZA), feat(ATT_WIDTH, ROW_ZB),
                  feat(D_MODEL, ROW_GA), feat(D_MODEL, ROW_GB),
                  whole((D_MODEL, ATT_WIDTH)), whole((D_MODEL, ATT_WIDTH)),
                  whole((D_MODEL, D_MODEL)), whole((1, D_MODEL))],
        out_specs=tok,
        out_shape=jax.ShapeDtypeStruct((b, n, D_MODEL), F32),
        compiler_params=pltpu.CompilerParams(
            dimension_semantics=("arbitrary", "arbitrary"), vmem_limit_bytes=VMEM_LIMIT),
        name="mix_out",
    )(x, oa, ob, pt, pt, pt, pt, wpa_t, wpb_t, wout, fg)


def _encode(x, meta_pk, meta_pt, params):
    n = x.shape[1]
    pk, pt = _norm_proj(x, params["norm_g"], params["w_k"], params["w_t"], PROJ_TM)
    oa = _na_attend(pk, pt, meta_pk, meta_pt, params["na_bias"])
    band, meta_bias, sink_row = _wa_bias_tables(params["t5_bias"], params["sink"], n // BLOCK)
    ob = _wa_attend(pk, pt, meta_pk, meta_pt, band, meta_bias, sink_row)
    return _mix_out(x, oa, ob, pt, params["w_proj_a_t"], params["w_proj_b_t"], params["w_out"],
                    params["final_g"], MIX_TM)


def kernel(x_prompt, x_sample, meta_tokens, norm_g, w_in, na_rpb, sink_logit, w_proj_a, w_proj_b,
           w_out, t5_bias, final_g):
    assert norm_g.shape[0] == 1, "one layer"
    w = w_in[0]
    scale = HEAD_DIM ** -0.5 * LOG2_E
    q_a, k_a, v_a, z_a = (w[:, i * 512:(i + 1) * 512] for i in range(4))
    q_b, k_b, v_b, z_b = w[:, 2048:2560], w[:, 2560:2688], w[:, 2688:2816], w[:, 2816:3328]
    g_a, g_b = w[:, 3328:4352], w[:, 4352:5376]
    w_t = jnp.concatenate([g_a, g_b, q_a * scale, v_a, z_a, q_b * scale, z_b, v_b], axis=1)
    params = {
        "norm_g": norm_g[0].reshape(1, D_MODEL).astype(F32),
        "w_k": jnp.concatenate([k_a, k_b], axis=1).astype(BF16),
        "w_t": jnp.transpose(w_t).astype(BF16),
        "na_bias": _na_bias_tables(na_rpb[0]),
        "t5_bias": t5_bias,
        "sink": sink_logit[0],
        "w_proj_a_t": jnp.transpose(w_proj_a[0]).astype(BF16),
        "w_proj_b_t": jnp.transpose(w_proj_b[0]).astype(BF16),
        "w_out": w_out[0].astype(BF16),
        "final_g": final_g.reshape(1, D_MODEL).astype(F32),
    }
    meta_pk, meta_pt = _norm_proj(meta_tokens.astype(F32)[None], params["norm_g"],
                                  params["w_k"], params["w_t"], N_META)
    return (_encode(x_prompt, meta_pk[0], meta_pt, params),
            _encode(x_sample, meta_pk[0], meta_pt, params))
```

```python
import math

import jax
import jax.numpy as jnp
from jax import lax
from jax.experimental import pallas as pl
from jax.experimental.pallas import tpu as pltpu

F32 = jnp.float32
BF16 = jnp.bfloat16

D_MODEL = 1024
N_META = 16
GRID_W = 64
HEADS = 8
HEAD_DIM = 64
ATT_WIDTH = HEADS * HEAD_DIM
NA_WIN_ROWS = 8
NA_WIN_COLS = 16
WA_KV_HEADS = 2
WA_GROUP = HEADS // WA_KV_HEADS
WA_KV_WIDTH = WA_KV_HEADS * HEAD_DIM
WINDOW = 128
BLOCK = 128
T5_BUCKETS = 32
T5_MAX_DIST = 128
RMS_EPS = 1e-6
NEG_INF = -1e30
LOG2_E = math.log2(math.e)

K_WIDTH = ATT_WIDTH + WA_KV_WIDTH
ROW_GA, ROW_GB, ROW_QA, ROW_VA, ROW_ZA, ROW_QB, ROW_ZB, ROW_VB = (
    0, 1024, 2048, 2560, 3072, 3584, 4096, 4608)
T_WIDTH = 4736

NA_ROWS_PER_STEP = 4
NA_Q = NA_ROWS_PER_STEP * GRID_W
NA_KEY_SEGS = 3
PAIR = 2 * HEAD_DIM
TOKEN_TILE = 256
PROJ_TM = 512
PROJ_ROW_CHUNK = 1184
MIX_TM = 512
VMEM_LIMIT = 52 * 1024 * 1024

_NT = (((1,), (1,)), ((), ()))
_TN = (((0,), (0,)), ((), ()))


def _t5_bucket(rel):
    half = T5_BUCKETS // 2
    exact = half // 2
    ret = jnp.where(rel > 0, half, 0)
    n = jnp.abs(rel)
    nf = jnp.maximum(n, 1).astype(F32)
    large = exact + (jnp.log(nf / exact) / math.log(T5_MAX_DIST / exact)
                     * (half - exact)).astype(jnp.int32)
    large = jnp.minimum(large, half - 1)
    return ret + jnp.where(n < exact, n, large)


def _t5_lookup(tab, rel):
    bucket = _t5_bucket(rel)
    tail = (1,) * bucket.ndim
    hit = bucket[None] == jnp.arange(T5_BUCKETS).reshape((T5_BUCKETS,) + tail)
    vals = jnp.transpose(tab).reshape((tab.shape[1], T5_BUCKETS) + tail)
    return jnp.sum(jnp.where(hit[None], vals, 0.0), axis=1)


def _toeplitz(v, rows, cols):
    length = rows + cols - 1
    assert v.shape[-1] == length
    lead = v.shape[:-1]
    flat = jnp.broadcast_to(v[..., None, :], lead + (rows, length)).reshape(lead + (rows * length,))
    flat = flat[..., rows - 1:rows - 1 + rows * (length - 1)]
    return flat.reshape(lead + (rows, length - 1))[..., :cols]


ONES_ROWS = 16


def _with_ones_rows(v):
    return jnp.concatenate([v, jnp.ones((ONES_ROWS, v.shape[1]), v.dtype)], axis=0)


def _half_padded(q, upper):
    zeros = jnp.zeros_like(q)
    return jnp.concatenate([zeros, q] if upper else [q, zeros], axis=0)


def _norm_proj_body(x_ref, g_ref, wk_ref, wt_ref, oka_ref, okb_ref, ot_ref):
    x = x_ref[...]
    ms = jnp.mean(x * x, axis=-1, keepdims=True)
    u = ((x * lax.rsqrt(ms + RMS_EPS)) * g_ref[...]).astype(BF16)
    keys = jnp.dot(u, wk_ref[...], preferred_element_type=F32).astype(BF16)
    oka_ref[...] = keys[:, :ATT_WIDTH]
    okb_ref[...] = keys[:, ATT_WIDTH:]
    n_tiles, _, tile = ot_ref.shape[1:]
    for r in range(0, T_WIDTH, PROJ_ROW_CHUNK):
        rows = lax.dot_general(wt_ref[r:r + PROJ_ROW_CHUNK, :], u, _NT,
                               preferred_element_type=F32).astype(BF16)
        for t in range(n_tiles):
            ot_ref[0, t, r:r + PROJ_ROW_CHUNK, :] = rows[:, t * tile:(t + 1) * tile]


def _norm_proj(x, g, wk, wt, tm, tile):
    b, n, _ = x.shape
    assert n % tm == 0 and tm % tile == 0
    return pl.pallas_call(
        _norm_proj_body,
        grid=(b, n // tm),
        in_specs=[pl.BlockSpec((None, tm, D_MODEL), lambda bb, i: (bb, i, 0)),
                  pl.BlockSpec((1, D_MODEL), lambda bb, i: (0, 0)),
                  pl.BlockSpec((D_MODEL, K_WIDTH), lambda bb, i: (0, 0)),
                  pl.BlockSpec((T_WIDTH, D_MODEL), lambda bb, i: (0, 0))],
        out_specs=[pl.BlockSpec((None, tm, ATT_WIDTH), lambda bb, i: (bb, i, 0)),
                   pl.BlockSpec((None, tm, WA_KV_WIDTH), lambda bb, i: (bb, i, 0)),
                   pl.BlockSpec((1, tm // tile, T_WIDTH, tile), lambda bb, i: (bb, i, 0, 0))],
        out_shape=[jax.ShapeDtypeStruct((b, n, ATT_WIDTH), BF16),
                   jax.ShapeDtypeStruct((b, n, WA_KV_WIDTH), BF16),
                   jax.ShapeDtypeStruct((b, n // tile, T_WIDTH, tile), BF16)],
        compiler_params=pltpu.CompilerParams(
            dimension_semantics=("arbitrary", "arbitrary"), vmem_limit_bytes=VMEM_LIMIT),
        name="norm_proj",
    )(x, g, wk, wt)


def _na_bias_tables(rpb):
    n_t = NA_KEY_SEGS * NA_ROWS_PER_STEP
    lead = GRID_W - NA_WIN_COLS
    v = jnp.pad(rpb.astype(F32) * LOG2_E, ((0, 0), (0, 0), (lead, lead)))
    cols = _toeplitz(v, GRID_W, GRID_W)
    j = jnp.arange(GRID_W)[:, None]
    c = jnp.arange(GRID_W)[None, :]
    cs = jnp.clip(j - NA_WIN_COLS // 2, 0, GRID_W - NA_WIN_COLS)
    cols = jnp.where((c >= cs) & (c < cs + NA_WIN_COLS), cols, NEG_INF)
    assert NA_WIN_ROWS // 2 - 1 == NA_ROWS_PER_STEP - 1 and cols.shape[1] == NA_ROWS_PER_STEP + n_t - 1
    full = _toeplitz(jnp.transpose(cols, (0, 3, 2, 1)), NA_ROWS_PER_STEP, n_t)
    full = jnp.transpose(full, (0, 4, 1, 3, 2))
    a = jnp.arange(NA_ROWS_PER_STEP)[None, :]
    t = jnp.arange(n_t)[:, None]
    row_ok = (
        (t >= NA_ROWS_PER_STEP) & (a >= 0),
        (t - a >= 0) & (t - a < NA_WIN_ROWS),
        (t < NA_WIN_ROWS) & (a >= 0),
    )
    tabs = [jnp.where(ok[None, :, None, :, None], full, NEG_INF) for ok in row_ok]
    return jnp.stack(tabs).reshape(3, HEADS, NA_KEY_SEGS * NA_Q, NA_Q)


NA_KEYS = NA_KEY_SEGS * NA_Q + N_META


def _na_body(q_ref, kp_ref, kc_ref, kn_ref, vp_ref, vc_ref, vn_ref, km_ref, vm_ref,
             bias_ref, o_ref, s0_ref, m0_ref, s1_ref, m1_ref):
    k_refs = (kp_ref, kc_ref, kn_ref)
    v_refs = (vp_ref, vc_ref, vn_ref)
    n_win = NA_KEY_SEGS * NA_Q
    step = pl.program_id(0) * pl.num_programs(1) + pl.program_id(1)

    @pl.when(step == 0)
    def _():
        s1_ref[...] = jnp.zeros_like(s1_ref)
        m1_ref[...] = jnp.zeros_like(m1_ref)

    def stages(s_prev, m_prev, s_next, m_next):
        for h in range(HEADS):
            rows = slice(h * HEAD_DIM, (h + 1) * HEAD_DIM)
            m = m_prev[h, 0:1, :]
            p_meta = jnp.exp2(s_prev[h, n_win:NA_KEYS, :] - m)
            o = jnp.dot(_with_ones_rows(vm_ref[rows, :]), p_meta.astype(BF16),
                        preferred_element_type=F32)
            for g, vr in enumerate(v_refs):
                p = jnp.exp2(s_prev[h, g * NA_Q:(g + 1) * NA_Q, :] - m)
                o = o + jnp.dot(_with_ones_rows(vr[rows, :]), p.astype(BF16),
                                preferred_element_type=F32)
            o_ref[rows, :] = (o[:HEAD_DIM] / o[HEAD_DIM:HEAD_DIM + 1]).astype(BF16)
            slab = slice((h // 2) * PAIR, (h // 2 + 1) * PAIR)
            q = _half_padded(q_ref[rows, :], h % 2)
            s_meta = jnp.dot(km_ref[:, slab], q, preferred_element_type=F32)
            s_next[h, n_win:NA_KEYS, :] = s_meta
            m = jnp.max(s_meta, axis=0, keepdims=True)
            for g, kr in enumerate(k_refs):
                sg = (jnp.dot(kr[:, slab], q, preferred_element_type=F32)
                      + bias_ref[0, h, g * NA_Q:(g + 1) * NA_Q, :])
                s_next[h, g * NA_Q:(g + 1) * NA_Q, :] = sg
                m = jnp.maximum(m, jnp.max(sg, axis=0, keepdims=True))
            m_next[h, 0:1, :] = m

    @pl.when(step % 2 == 0)
    def _():
        stages(s1_ref, m1_ref, s0_ref, m0_ref)

    @pl.when(step % 2 == 1)
    def _():
        stages(s0_ref, m0_ref, s1_ref, m1_ref)


def _na_attend(ka, pt, meta_ka, meta_pt, bias_tabs):
    b, n, _ = ka.shape
    assert pt.shape[1:] == (n // NA_Q, T_WIDTH, NA_Q)
    nblk = n // NA_Q
    assert nblk >= 3
    rq, rv = ROW_QA // ATT_WIDTH, ROW_VA // ATT_WIDTH

    def scored(i):
        return jnp.minimum(i, nblk - 1)

    def finished(i):
        return jnp.maximum(i - 1, 0)

    def shifted(blk, shift):
        return jnp.clip(blk + shift, 0, nblk - 1)

    def kseg(shift):
        return pl.BlockSpec((None, NA_Q, ATT_WIDTH),
                            lambda bb, i: (bb, shifted(scored(i), shift), 0))

    def vseg(shift):
        return pl.BlockSpec((None, None, ATT_WIDTH, NA_Q),
                            lambda bb, i: (bb, shifted(finished(i), shift), rv, 0))

    def step_kind(bb, i):
        blk = scored(i)
        return (jnp.where(blk == 0, 0, jnp.where(blk == nblk - 1, 2, 1)), 0, 0, 0)

    return pl.pallas_call(
        _na_body,
        grid=(b, nblk + 1),
        in_specs=[pl.BlockSpec((None, None, ATT_WIDTH, NA_Q),
                               lambda bb, i: (bb, scored(i), rq, 0)),
                  kseg(-1), kseg(0), kseg(1),
                  vseg(-1), vseg(0), vseg(1),
                  pl.BlockSpec((None, N_META, ATT_WIDTH), lambda bb, i: (0, 0, 0)),
                  pl.BlockSpec((None, None, ATT_WIDTH, N_META), lambda bb, i: (0, 0, rv, 0)),
                  pl.BlockSpec((1, HEADS, NA_KEY_SEGS * NA_Q, NA_Q), step_kind)],
        out_specs=pl.BlockSpec((None, None, ATT_WIDTH, NA_Q),
                               lambda bb, i: (bb, finished(i), 0, 0)),
        out_shape=jax.ShapeDtypeStruct((b, nblk, ATT_WIDTH, NA_Q), BF16),
        scratch_shapes=[pltpu.VMEM((HEADS, NA_KEYS, NA_Q), F32),
                        pltpu.VMEM((HEADS, 8, NA_Q), F32)] * 2,
        compiler_params=pltpu.CompilerParams(
            dimension_semantics=("arbitrary", "arbitrary"), vmem_limit_bytes=VMEM_LIMIT),
        name="na_attend",
    )(pt, ka, ka, ka, pt, pt, pt, meta_ka, meta_pt, bias_tabs)


def _wa_bias_tables(t5_bias, sink, nb):
    tab = t5_bias.astype(F32) * LOG2_E
    qq = jnp.arange(BLOCK)
    kk = jnp.arange(3 * BLOCK)
    rel = kk[None, :] - BLOCK - qq[:, None]
    dist = jnp.arange(-(2 * BLOCK - 1), 2 * BLOCK)
    band = _toeplitz(_t5_lookup(tab, dist), BLOCK, 3 * BLOCK)
    in_window = jnp.abs(rel) <= WINDOW
    seg = (kk // BLOCK)[None, :]
    kinds = (in_window & (seg != 0), in_window, in_window & (seg != 2))
    band = jnp.stack([jnp.where(ok[None], band, NEG_INF) for ok in kinds])
    band = band.reshape(3, WA_KV_HEADS, WA_GROUP, BLOCK, 3 * BLOCK)
    band = jnp.transpose(band, (0, 1, 4, 2, 3)).reshape(3, WA_KV_HEADS, 3 * BLOCK, WA_GROUP * BLOCK)
    n = nb * BLOCK
    per_dist = _t5_lookup(tab, -jnp.arange(n + BLOCK + N_META))
    main = per_dist[:, 1:1 + n].reshape(HEADS, nb, BLOCK)
    tail = per_dist[:, 1 + BLOCK:1 + BLOCK + n].reshape(HEADS, nb, BLOCK)[..., :N_META - 1]
    meta = _toeplitz(jnp.concatenate([main, tail], axis=-1), N_META, BLOCK)
    meta = meta.reshape(WA_KV_HEADS, WA_GROUP, nb, N_META, BLOCK)
    meta = jnp.transpose(meta, (2, 0, 3, 1, 4)).reshape(nb, WA_KV_HEADS, N_META, WA_GROUP * BLOCK)
    sink_row = jnp.broadcast_to((sink.astype(F32) * LOG2_E).reshape(WA_KV_HEADS, WA_GROUP, 1),
                                (WA_KV_HEADS, WA_GROUP, BLOCK)).reshape(WA_KV_HEADS, 1, WA_GROUP * BLOCK)
    return band, meta, sink_row


WA_KEYS = 3 * BLOCK + N_META


WA_SUB = 2
WA_Q = WA_SUB * BLOCK


def _wa_body(q_ref, kp_ref, kc_ref, kn_ref, vp_ref, vc_ref, vn_ref, km_ref, vm_ref,
             band_a_ref, band_b_ref, mbias_ref, sink_ref, o_ref, s0_ref, m0_ref, s1_ref, m1_ref):
    n_band = 3 * BLOCK
    lo, hi = slice(0, BLOCK), slice(BLOCK, WA_Q)
    key_halves = (((kp_ref, hi), (kc_ref, lo), (kc_ref, hi)),
                  ((kc_ref, lo), (kc_ref, hi), (kn_ref, lo)))
    val_halves = (((vp_ref, hi), (vc_ref, lo), (vc_ref, hi)),
                  ((vc_ref, lo), (vc_ref, hi), (vn_ref, lo)))
    band_refs = (band_a_ref, band_b_ref)
    step = pl.program_id(0) * pl.num_programs(1) + pl.program_id(1)

    @pl.when(step == 0)
    def _():
        s1_ref[...] = jnp.zeros_like(s1_ref)
        m1_ref[...] = jnp.zeros_like(m1_ref)

    def stages(s_prev, m_prev, s_next, m_next):
        for sub in range(WA_SUB):
            cols = slice(sub * BLOCK, (sub + 1) * BLOCK)
            for kv in range(WA_KV_HEADS):
                vrows = slice(kv * HEAD_DIM, (kv + 1) * HEAD_DIM)
                m = m_prev[sub, kv, 0:1, :]
                p_meta = jnp.exp2(s_prev[sub, kv, n_band:WA_KEYS, :] - m)
                o = jnp.dot(_with_ones_rows(vm_ref[vrows, :]), p_meta.astype(BF16),
                            preferred_element_type=F32)
                for g, (vr, half) in enumerate(val_halves[sub]):
                    p = jnp.exp2(s_prev[sub, kv, g * BLOCK:(g + 1) * BLOCK, :] - m)
                    o = o + jnp.dot(_with_ones_rows(vr[vrows, half]), p.astype(BF16),
                                    preferred_element_type=F32)
                l = o[HEAD_DIM:HEAD_DIM + 1] + jnp.exp2(sink_ref[kv] - m)
                o = (o[:HEAD_DIM] / l).astype(BF16)
                for g in range(WA_GROUP):
                    h = kv * WA_GROUP + g
                    o_ref[h * HEAD_DIM:(h + 1) * HEAD_DIM, cols] = o[:, g * BLOCK:(g + 1) * BLOCK]
                q4 = jnp.concatenate(
                    [q_ref[(kv * WA_GROUP + g) * HEAD_DIM:(kv * WA_GROUP + g + 1) * HEAD_DIM, cols]
                     for g in range(WA_GROUP)], axis=1)
                q4 = _half_padded(q4, kv)
                s_meta = (jnp.dot(km_ref[...], q4, preferred_element_type=F32)
                          + mbias_ref[sub, kv])
                s_next[sub, kv, n_band:WA_KEYS, :] = s_meta
                m = jnp.maximum(jnp.max(s_meta, axis=0, keepdims=True), sink_ref[kv])
                for g, (kr, half) in enumerate(key_halves[sub]):
                    sg = (jnp.dot(kr[half, :], q4, preferred_element_type=F32)
                          + band_refs[sub][0, kv, g * BLOCK:(g + 1) * BLOCK, :])
                    s_next[sub, kv, g * BLOCK:(g + 1) * BLOCK, :] = sg
                    m = jnp.maximum(m, jnp.max(sg, axis=0, keepdims=True))
                m_next[sub, kv, 0:1, :] = m

    @pl.when(step % 2 == 0)
    def _():
        stages(s1_ref, m1_ref, s0_ref, m0_ref)

    @pl.when(step % 2 == 1)
    def _():
        stages(s0_ref, m0_ref, s1_ref, m1_ref)


def _wa_attend(kb, pt, meta_kb, meta_pt, band, meta_bias, sink_row):
    b, n, _ = kb.shape
    assert pt.shape[1:] == (n // WA_Q, T_WIDTH, WA_Q)
    nt = n // WA_Q
    assert nt >= 2
    rq, rv = ROW_QB // ATT_WIDTH, ROW_VB // WA_KV_WIDTH

    def scored(i):
        return jnp.minimum(i, nt - 1)

    def finished(i):
        return jnp.maximum(i - 1, 0)

    def shifted(tile, shift):
        return jnp.clip(tile + shift, 0, nt - 1)

    def kseg(shift):
        return pl.BlockSpec((None, WA_Q, WA_KV_WIDTH),
                            lambda bb, i: (bb, shifted(scored(i), shift), 0))

    def vseg(shift):
        return pl.BlockSpec((None, None, WA_KV_WIDTH, WA_Q),
                            lambda bb, i: (bb, shifted(finished(i), shift), rv, 0))

    def kind_a(bb, i):
        return (jnp.where(scored(i) == 0, 0, 1), 0, 0, 0)

    def kind_b(bb, i):
        return (jnp.where(scored(i) == nt - 1, 2, 1), 0, 0, 0)

    lanes = WA_GROUP * BLOCK
    band_spec = (1, WA_KV_HEADS, 3 * BLOCK, lanes)
    return pl.pallas_call(
        _wa_body,
        grid=(b, nt + 1),
        in_specs=[pl.BlockSpec((None, None, ATT_WIDTH, WA_Q),
                               lambda bb, i: (bb, scored(i), rq, 0)),
                  kseg(-1), kseg(0), kseg(1),
                  vseg(-1), vseg(0), vseg(1),
                  pl.BlockSpec((None, N_META, WA_KV_WIDTH), lambda bb, i: (0, 0, 0)),
                  pl.BlockSpec((None, None, WA_KV_WIDTH, N_META), lambda bb, i: (0, 0, rv, 0)),
                  pl.BlockSpec(band_spec, kind_a), pl.BlockSpec(band_spec, kind_b),
                  pl.BlockSpec((WA_SUB, WA_KV_HEADS, N_META, lanes),
                               lambda bb, i: (scored(i), 0, 0, 0)),
                  pl.BlockSpec((WA_KV_HEADS, 1, lanes), lambda bb, i: (0, 0, 0))],
        out_specs=pl.BlockSpec((None, None, ATT_WIDTH, WA_Q),
                               lambda bb, i: (bb, finished(i), 0, 0)),
        out_shape=jax.ShapeDtypeStruct((b, nt, ATT_WIDTH, WA_Q), BF16),
        scratch_shapes=[pltpu.VMEM((WA_SUB, WA_KV_HEADS, WA_KEYS, lanes), F32),
                        pltpu.VMEM((WA_SUB, WA_KV_HEADS, 8, lanes), F32)] * 2,
        compiler_params=pltpu.CompilerParams(
            dimension_semantics=("arbitrary", "arbitrary"), vmem_limit_bytes=VMEM_LIMIT),
        name="wa_attend",
    )(pt, kb, kb, kb, pt, pt, pt, meta_kb, meta_pt, band, band, meta_bias, sink_row)


def _sigmoid(x):
    return 0.5 * jnp.tanh(0.5 * x) + 0.5


def _tiles_side_by_side(ref):
    return jnp.concatenate([ref[t] for t in range(ref.shape[0])], axis=1)


def _mix_out_body(x_ref, oa_ref, ob_ref, za_ref, zb_ref, ga_ref, gb_ref,
                  wpa_ref, wpb_ref, wout_ref, fg_ref, y_ref):
    za = _tiles_side_by_side(za_ref).astype(F32)
    zb = _tiles_side_by_side(zb_ref).astype(F32)
    ta = (_tiles_side_by_side(oa_ref).astype(F32) * (za * _sigmoid(za))).astype(BF16)
    tb = (_tiles_side_by_side(ob_ref).astype(F32) * (zb * _sigmoid(zb))).astype(BF16)
    ya = jnp.dot(wpa_ref[...], ta, preferred_element_type=F32)
    yb = jnp.dot(wpb_ref[...], tb, preferred_element_type=F32)
    merged = (_sigmoid(_tiles_side_by_side(ga_ref).astype(F32)) * ya
              + _sigmoid(_tiles_side_by_side(gb_ref).astype(F32)) * yb).astype(BF16)
    h = x_ref[0] + lax.dot_general(merged, wout_ref[...], _TN, preferred_element_type=F32)
    ms = jnp.mean(h * h, axis=-1, keepdims=True)
    y_ref[0] = (h * lax.rsqrt(ms + RMS_EPS)) * fg_ref[...]


def _mix_out(x, oa, ob, pt, wpa_t, wpb_t, wout, fg, tm):
    b, n, _ = x.shape
    tile = pt.shape[-1]
    assert n % tm == 0 and tm % tile == 0 and oa.shape == ob.shape == (b, n // tile, ATT_WIDTH, tile)

    def feat(rows, row0):
        return pl.BlockSpec((None, tm // tile, rows, tile),
                            lambda bb, i: (bb, i, row0 // rows, 0))

    def whole(shape):
        return pl.BlockSpec(shape, lambda bb, i: (0, 0))

    tok = pl.BlockSpec((1, tm, D_MODEL), lambda bb, i: (bb, i, 0))
    return pl.pallas_call(
        _mix_out_body,
        grid=(b, n // tm),
        in_specs=[tok, feat(ATT_WIDTH, 0), feat(ATT_WIDTH, 0),
                  feat(ATT_WIDTH, ROW_ZA), feat(ATT_WIDTH, ROW_ZB),
                  feat(D_MODEL, ROW_GA), feat(D_MODEL, ROW_GB),
                  whole((D_MODEL, ATT_WIDTH)), whole((D_MODEL, ATT_WIDTH)),
                  whole((D_MODEL, D_MODEL)), whole((1, D_MODEL))],
        out_specs=tok,
        out_shape=jax.ShapeDtypeStruct((b, n, D_MODEL), F32),
        compiler_params=pltpu.CompilerParams(
            dimension_semantics=("arbitrary", "arbitrary"), vmem_limit_bytes=VMEM_LIMIT),
        name="mix_out",
    )(x, oa, ob, pt, pt, pt, pt, wpa_t, wpb_t, wout, fg)


def _encode(x, meta, params):
    n = x.shape[1]
    meta_ka, meta_kb, meta_pt = meta
    ka, kb, pt = _norm_proj(x, params["norm_g"], params["w_k"], params["w_t"], PROJ_TM, TOKEN_TILE)
    oa = _na_attend(ka, pt, meta_ka, meta_pt, params["na_bias"])
    band, meta_bias, sink_row = _wa_bias_tables(params["t5_bias"], params["sink"], n // BLOCK)
    ob = _wa_attend(kb, pt, meta_kb, meta_pt, band, meta_bias, sink_row)
    return _mix_out(x, oa, ob, pt, params["w_proj_a_t"], params["w_proj_b_t"], params["w_out"],
                    params["final_g"], MIX_TM)


def kernel(x_prompt, x_sample, meta_tokens, norm_g, w_in, na_rpb, sink_logit, w_proj_a, w_proj_b,
           w_out, t5_bias, final_g):
    assert norm_g.shape[0] == 1, "one layer"
    w = w_in[0]
    scale = HEAD_DIM ** -0.5 * LOG2_E
    q_a, k_a, v_a, z_a = (w[:, i * 512:(i + 1) * 512] for i in range(4))
    q_b, k_b, v_b, z_b = w[:, 2048:2560], w[:, 2560:2688], w[:, 2688:2816], w[:, 2816:3328]
    g_a, g_b = w[:, 3328:4352], w[:, 4352:5376]
    w_t = jnp.concatenate([g_a, g_b, q_a * scale, v_a, z_a, q_b * scale, z_b, v_b], axis=1)
    params = {
        "norm_g": norm_g[0].reshape(1, D_MODEL).astype(F32),
        "w_k": jnp.concatenate([k_a, k_b], axis=1).astype(BF16),
        "w_t": jnp.transpose(w_t).astype(BF16),
        "na_bias": _na_bias_tables(na_rpb[0]),
        "t5_bias": t5_bias,
        "sink": sink_logit[0],
        "w_proj_a_t": jnp.transpose(w_proj_a[0]).astype(BF16),
        "w_proj_b_t": jnp.transpose(w_proj_b[0]).astype(BF16),
        "w_out": w_out[0].astype(BF16),
        "final_g": final_g.reshape(1, D_MODEL).astype(F32),
    }
    meta = _norm_proj(meta_tokens.astype(F32)[None], params["norm_g"],
                      params["w_k"], params["w_t"], N_META, N_META)
    return (_encode(x_prompt, meta, params), _encode(x_sample, meta, params))
```

```python
import math

import jax
import jax.numpy as jnp
from jax import lax
from jax.experimental import pallas as pl
from jax.experimental.pallas import tpu as pltpu

F32 = jnp.float32
BF16 = jnp.bfloat16

D_MODEL = 1024
N_META = 16
GRID_W = 64
HEADS = 8
HEAD_DIM = 64
ATT_WIDTH = HEADS * HEAD_DIM
NA_WIN_ROWS = 8
NA_WIN_COLS = 16
WA_KV_HEADS = 2
WA_GROUP = HEADS // WA_KV_HEADS
WA_KV_WIDTH = WA_KV_HEADS * HEAD_DIM
WINDOW = 128
BLOCK = 128
T5_BUCKETS = 32
T5_MAX_DIST = 128
RMS_EPS = 1e-6
NEG_INF = -1e30
LOG2_E = math.log2(math.e)

K_WIDTH = ATT_WIDTH + WA_KV_WIDTH
ROW_GA, ROW_GB, ROW_QA, ROW_VA, ROW_ZA, ROW_QB, ROW_ZB, ROW_VB = (
    0, 1024, 2048, 2560, 3072, 3584, 4096, 4608)
T_WIDTH = 4736

NA_ROWS_PER_STEP = 4
NA_Q = NA_ROWS_PER_STEP * GRID_W
NA_KEY_SEGS = 3
PAIR = 2 * HEAD_DIM
TOKEN_TILE = 256
PROJ_TM = 512
PROJ_ROW_CHUNK = 1184
MIX_TM = 512
VMEM_LIMIT = 52 * 1024 * 1024

_NT = (((1,), (1,)), ((), ()))
_TN = (((0,), (0,)), ((), ()))


def _t5_bucket(rel):
    half = T5_BUCKETS // 2
    exact = half // 2
    ret = jnp.where(rel > 0, half, 0)
    n = jnp.abs(rel)
    nf = jnp.maximum(n, 1).astype(F32)
    large = exact + (jnp.log(nf / exact) / math.log(T5_MAX_DIST / exact)
                     * (half - exact)).astype(jnp.int32)
    large = jnp.minimum(large, half - 1)
    return ret + jnp.where(n < exact, n, large)


def _t5_lookup(tab, rel):
    bucket = _t5_bucket(rel)
    tail = (1,) * bucket.ndim
    hit = bucket[None] == jnp.arange(T5_BUCKETS).reshape((T5_BUCKETS,) + tail)
    vals = jnp.transpose(tab).reshape((tab.shape[1], T5_BUCKETS) + tail)
    return jnp.sum(jnp.where(hit[None], vals, 0.0), axis=1)


def _toeplitz(v, rows, cols):
    length = rows + cols - 1
    assert v.shape[-1] == length
    lead = v.shape[:-1]
    flat = jnp.broadcast_to(v[..., None, :], lead + (rows, length)).reshape(lead + (rows * length,))
    flat = flat[..., rows - 1:rows - 1 + rows * (length - 1)]
    return flat.reshape(lead + (rows, length - 1))[..., :cols]


ONES_ROWS = 16


def _with_ones_rows(v):
    return jnp.concatenate([v, jnp.ones((ONES_ROWS, v.shape[1]), v.dtype)], axis=0)


def _half_padded(q, upper):
    zeros = jnp.zeros_like(q)
    return jnp.concatenate([zeros, q] if upper else [q, zeros], axis=0)


def _norm_proj_body(x_ref, g_ref, wk_ref, wt_ref, oka_ref, okb_ref, ot_ref):
    x = x_ref[...]
    ms = jnp.mean(x * x, axis=-1, keepdims=True)
    u = ((x * lax.rsqrt(ms + RMS_EPS)) * g_ref[...]).astype(BF16)
    keys = jnp.dot(u, wk_ref[...], preferred_element_type=F32).astype(BF16)
    oka_ref[...] = keys[:, :ATT_WIDTH]
    okb_ref[...] = keys[:, ATT_WIDTH:]
    n_tiles, _, tile = ot_ref.shape[1:]
    for r in range(0, T_WIDTH, PROJ_ROW_CHUNK):
        rows = lax.dot_general(wt_ref[r:r + PROJ_ROW_CHUNK, :], u, _NT,
                               preferred_element_type=F32).astype(BF16)
        for t in range(n_tiles):
            ot_ref[0, t, r:r + PROJ_ROW_CHUNK, :] = rows[:, t * tile:(t + 1) * tile]


def _norm_proj(x, g, wk, wt, tm, tile):
    b, n, _ = x.shape
    assert n % tm == 0 and tm % tile == 0
    return pl.pallas_call(
        _norm_proj_body,
        grid=(b, n // tm),
        in_specs=[pl.BlockSpec((None, tm, D_MODEL), lambda bb, i: (bb, i, 0)),
                  pl.BlockSpec((1, D_MODEL), lambda bb, i: (0, 0)),
                  pl.BlockSpec((D_MODEL, K_WIDTH), lambda bb, i: (0, 0)),
                  pl.BlockSpec((T_WIDTH, D_MODEL), lambda bb, i: (0, 0))],
        out_specs=[pl.BlockSpec((None, tm, ATT_WIDTH), lambda bb, i: (bb, i, 0)),
                   pl.BlockSpec((None, tm, WA_KV_WIDTH), lambda bb, i: (bb, i, 0)),
                   pl.BlockSpec((1, tm // tile, T_WIDTH, tile), lambda bb, i: (bb, i, 0, 0))],
        out_shape=[jax.ShapeDtypeStruct((b, n, ATT_WIDTH), BF16),
                   jax.ShapeDtypeStruct((b, n, WA_KV_WIDTH), BF16),
                   jax.ShapeDtypeStruct((b, n // tile, T_WIDTH, tile), BF16)],
        compiler_params=pltpu.CompilerParams(
            dimension_semantics=("arbitrary", "arbitrary"), vmem_limit_bytes=VMEM_LIMIT),
        name="norm_proj",
    )(x, g, wk, wt)


def _na_bias_tables(rpb):
    n_t = NA_KEY_SEGS * NA_ROWS_PER_STEP
    lead = GRID_W - NA_WIN_COLS
    v = jnp.pad(rpb.astype(F32) * LOG2_E, ((0, 0), (0, 0), (lead, lead)))
    cols = _toeplitz(v, GRID_W, GRID_W)
    j = jnp.arange(GRID_W)[:, None]
    c = jnp.arange(GRID_W)[None, :]
    cs = jnp.clip(j - NA_WIN_COLS // 2, 0, GRID_W - NA_WIN_COLS)
    cols = jnp.where((c >= cs) & (c < cs + NA_WIN_COLS), cols, NEG_INF)
    assert NA_WIN_ROWS // 2 - 1 == NA_ROWS_PER_STEP - 1 and cols.shape[1] == NA_ROWS_PER_STEP + n_t - 1
    full = _toeplitz(jnp.transpose(cols, (0, 3, 2, 1)), NA_ROWS_PER_STEP, n_t)
    full = jnp.transpose(full, (0, 4, 1, 3, 2))
    a = jnp.arange(NA_ROWS_PER_STEP)[None, :]
    t = jnp.arange(NA_WIN_T)[:, None]

    def masked(tab, ok):
        return jnp.where(ok[None, :, None, :, None], tab, NEG_INF)

    first8 = (t < NA_WIN_ROWS) & (a >= 0)
    top = jnp.concatenate([full[:, 2 * NA_ROWS_PER_STEP:], full[:, NA_ROWS_PER_STEP:NA_WIN_T]], axis=1)
    tabs = [masked(top, first8),
            masked(full[:, :NA_WIN_T], (t - a >= 0) & (t - a < NA_WIN_ROWS)),
            masked(full[:, :NA_WIN_T], first8)]
    return jnp.stack(tabs).reshape(3, HEADS, NA_WIN_KEYS, NA_Q)


NA_WIN_T = NA_KEY_SEGS * NA_ROWS_PER_STEP - 1
NA_WIN_KEYS = NA_WIN_T * GRID_W
NA_SEG_KEYS = (NA_Q, NA_Q, NA_WIN_KEYS - 2 * NA_Q)
NA_KEYS = NA_WIN_KEYS + N_META


def _na_body(q_ref, kp_ref, kc_ref, kn_ref, vp_ref, vc_ref, vn_ref, km_ref, vm_ref,
             bias_ref, o_ref, s0_ref, m0_ref, s1_ref, m1_ref):
    k_refs = (kp_ref, kc_ref, kn_ref)
    v_refs = (vp_ref, vc_ref, vn_ref)
    n_win = NA_WIN_KEYS
    step = pl.program_id(0)

    @pl.when(step == 0)
    def _():
        s1_ref[...] = jnp.zeros_like(s1_ref)
        m1_ref[...] = jnp.zeros_like(m1_ref)

    def stages(s_prev, m_prev, s_next, m_next):
        for h in range(HEADS):
            rows = slice(h * HEAD_DIM, (h + 1) * HEAD_DIM)
            m_old = m_prev[h, 0:1, :]
            p_meta = jnp.exp2(s_prev[h, n_win:NA_KEYS, :] - m_old)
            o = jnp.dot(_with_ones_rows(vm_ref[rows, :]), p_meta.astype(BF16),
                        preferred_element_type=F32)
            slab = slice((h // 2) * PAIR, (h // 2 + 1) * PAIR)
            q = _half_padded(q_ref[rows, :], h % 2)
            s_meta = jnp.dot(km_ref[:, slab], q, preferred_element_type=F32)
            s_next[h, n_win:NA_KEYS, :] = s_meta
            m = jnp.max(s_meta, axis=0, keepdims=True)
            for g, (kr, vr) in enumerate(zip(k_refs, v_refs)):
                used = NA_SEG_KEYS[g]
                seg = slice(g * NA_Q, g * NA_Q + used)
                p = jnp.exp2(s_prev[h, seg, :] - m_old).astype(BF16)
                if used < NA_Q:
                    p = jnp.concatenate([p, jnp.zeros((NA_Q - used, NA_Q), BF16)], axis=0)
                o = o + jnp.dot(_with_ones_rows(vr[rows, :]), p, preferred_element_type=F32)
                sg = (jnp.dot(kr[:used, slab], q, preferred_element_type=F32)
                      + bias_ref[0, h, seg, :])
                s_next[h, seg, :] = sg
                m = jnp.maximum(m, jnp.max(sg, axis=0, keepdims=True))
            o_ref[rows, :] = (o[:HEAD_DIM] / o[HEAD_DIM:HEAD_DIM + 1]).astype(BF16)
            m_next[h, 0:1, :] = m

    @pl.when(step % 2 == 0)
    def _():
        stages(s1_ref, m1_ref, s0_ref, m0_ref)

    @pl.when(step % 2 == 1)
    def _():
        stages(s0_ref, m0_ref, s1_ref, m1_ref)


def _na_attend(ka, pt, meta_ka, meta_pt, bias_tabs):
    b, n, _ = ka.shape
    assert pt.shape[1:] == (n // NA_Q, T_WIDTH, NA_Q)
    nblk = n // NA_Q
    assert nblk >= 3
    rq, rv = ROW_QA // ATT_WIDTH, ROW_VA // ATT_WIDTH

    steps = b * nblk

    def scored(s):
        return jnp.divmod(jnp.minimum(s, steps - 1), nblk)

    def finished(s):
        return jnp.divmod(jnp.maximum(s - 1, 0), nblk)

    def tile_of(blk, seg):
        return (jnp.where(blk == 0, 1, blk - 1), blk, jnp.minimum(blk + 1, nblk - 1))[seg]

    def kseg(seg):
        def index(s):
            bb, blk = scored(s)
            return bb, tile_of(blk, seg), 0
        return pl.BlockSpec((None, NA_Q, ATT_WIDTH), index)

    def vseg(seg):
        def index(s):
            bb, blk = finished(s)
            return bb, tile_of(blk, seg), rv, 0
        return pl.BlockSpec((None, None, ATT_WIDTH, NA_Q), index)

    def q_index(s):
        bb, blk = scored(s)
        return bb, blk, rq, 0

    def o_index(s):
        bb, blk = finished(s)
        return bb, blk, 0, 0

    def step_kind(s):
        _, blk = scored(s)
        return (jnp.where(blk == 0, 0, jnp.where(blk == nblk - 1, 2, 1)), 0, 0, 0)

    return pl.pallas_call(
        _na_body,
        grid=(steps + 1,),
        in_specs=[pl.BlockSpec((None, None, ATT_WIDTH, NA_Q), q_index),
                  kseg(0), kseg(1), kseg(2),
                  vseg(0), vseg(1), vseg(2),
                  pl.BlockSpec((None, N_META, ATT_WIDTH), lambda s: (0, 0, 0)),
                  pl.BlockSpec((None, None, ATT_WIDTH, N_META), lambda s: (0, 0, rv, 0)),
                  pl.BlockSpec((1, HEADS, NA_WIN_KEYS, NA_Q), step_kind)],
        out_specs=pl.BlockSpec((None, None, ATT_WIDTH, NA_Q), o_index),
        out_shape=jax.ShapeDtypeStruct((b, nblk, ATT_WIDTH, NA_Q), BF16),
        scratch_shapes=[pltpu.VMEM((HEADS, NA_KEYS, NA_Q), F32),
                        pltpu.VMEM((HEADS, 8, NA_Q), F32)] * 2,
        compiler_params=pltpu.CompilerParams(
            dimension_semantics=("arbitrary",), vmem_limit_bytes=VMEM_LIMIT),
        name="na_attend",
    )(pt, ka, ka, ka, pt, pt, pt, meta_ka, meta_pt, bias_tabs)


def _wa_bias_tables(t5_bias, sink, nb):
    tab = t5_bias.astype(F32) * LOG2_E
    qq = jnp.arange(BLOCK)
    kk = jnp.arange(3 * BLOCK)
    rel = kk[None, :] - BLOCK - qq[:, None]
    dist = jnp.arange(-(2 * BLOCK - 1), 2 * BLOCK)
    band = _toeplitz(_t5_lookup(tab, dist), BLOCK, 3 * BLOCK)
    in_window = jnp.abs(rel) <= WINDOW
    seg = (kk // BLOCK)[None, :]
    kinds = (in_window & (seg != 0), in_window, in_window & (seg != 2))
    band = jnp.stack([jnp.where(ok[None], band, NEG_INF) for ok in kinds])
    band = band.reshape(3, WA_KV_HEADS, WA_GROUP, BLOCK, 3 * BLOCK)
    band = jnp.transpose(band, (0, 1, 4, 2, 3)).reshape(3, WA_KV_HEADS, 3 * BLOCK, WA_GROUP * BLOCK)
    n = nb * BLOCK
    per_dist = _t5_lookup(tab, -jnp.arange(n + BLOCK + N_META))
    main = per_dist[:, 1:1 + n].reshape(HEADS, nb, BLOCK)
    tail = per_dist[:, 1 + BLOCK:1 + BLOCK + n].reshape(HEADS, nb, BLOCK)[..., :N_META - 1]
    meta = _toeplitz(jnp.concatenate([main, tail], axis=-1), N_META, BLOCK)
    meta = meta.reshape(WA_KV_HEADS, WA_GROUP, nb, N_META, BLOCK)
    meta = jnp.transpose(meta, (2, 0, 3, 1, 4)).reshape(nb, WA_KV_HEADS, N_META, WA_GROUP * BLOCK)
    sink_row = jnp.broadcast_to((sink.astype(F32) * LOG2_E).reshape(WA_KV_HEADS, WA_GROUP, 1),
                                (WA_KV_HEADS, WA_GROUP, BLOCK)).reshape(WA_KV_HEADS, 1, WA_GROUP * BLOCK)
    return band, meta, sink_row


WA_KEYS = 3 * BLOCK + N_META


WA_SUB = 2
WA_Q = WA_SUB * BLOCK


def _wa_body(q_ref, kp_ref, kc_ref, kn_ref, vp_ref, vc_ref, vn_ref, km_ref, vm_ref,
             band_a_ref, band_b_ref, mbias_ref, sink_ref, o_ref, s0_ref, m0_ref, s1_ref, m1_ref):
    n_band = 3 * BLOCK
    lo, hi = slice(0, BLOCK), slice(BLOCK, WA_Q)
    key_halves = (((kp_ref, hi), (kc_ref, lo), (kc_ref, hi)),
                  ((kc_ref, lo), (kc_ref, hi), (kn_ref, lo)))
    val_halves = (((vp_ref, hi), (vc_ref, lo), (vc_ref, hi)),
                  ((vc_ref, lo), (vc_ref, hi), (vn_ref, lo)))
    band_refs = (band_a_ref, band_b_ref)
    step = pl.program_id(0)

    @pl.when(step == 0)
    def _():
        s1_ref[...] = jnp.zeros_like(s1_ref)
        m1_ref[...] = jnp.zeros_like(m1_ref)

    def stages(s_prev, m_prev, s_next, m_next):
        for sub in range(WA_SUB):
            cols = slice(sub * BLOCK, (sub + 1) * BLOCK)
            for kv in range(WA_KV_HEADS):
                vrows = slice(kv * HEAD_DIM, (kv + 1) * HEAD_DIM)
                m = m_prev[sub, kv, 0:1, :]
                p_meta = jnp.exp2(s_prev[sub, kv, n_band:WA_KEYS, :] - m)
                o = jnp.dot(_with_ones_rows(vm_ref[vrows, :]), p_meta.astype(BF16),
                            preferred_element_type=F32)
                for g, (vr, half) in enumerate(val_halves[sub]):
                    p = jnp.exp2(s_prev[sub, kv, g * BLOCK:(g + 1) * BLOCK, :] - m)
                    o = o + jnp.dot(_with_ones_rows(vr[vrows, half]), p.astype(BF16),
                                    preferred_element_type=F32)
                l = o[HEAD_DIM:HEAD_DIM + 1] + jnp.exp2(sink_ref[kv] - m)
                o = (o[:HEAD_DIM] / l).astype(BF16)
                for g in range(WA_GROUP):
                    h = kv * WA_GROUP + g
                    o_ref[h * HEAD_DIM:(h + 1) * HEAD_DIM, cols] = o[:, g * BLOCK:(g + 1) * BLOCK]
                q4 = jnp.concatenate(
                    [q_ref[(kv * WA_GROUP + g) * HEAD_DIM:(kv * WA_GROUP + g + 1) * HEAD_DIM, cols]
                     for g in range(WA_GROUP)], axis=1)
                q4 = _half_padded(q4, kv)
                s_meta = (jnp.dot(km_ref[...], q4, preferred_element_type=F32)
                          + mbias_ref[sub, kv])
                s_next[sub, kv, n_band:WA_KEYS, :] = s_meta
                m = jnp.maximum(jnp.max(s_meta, axis=0, keepdims=True), sink_ref[kv])
                for g, (kr, half) in enumerate(key_halves[sub]):
                    sg = (jnp.dot(kr[half, :], q4, preferred_element_type=F32)
                          + band_refs[sub][0, kv, g * BLOCK:(g + 1) * BLOCK, :])
                    s_next[sub, kv, g * BLOCK:(g + 1) * BLOCK, :] = sg
                    m = jnp.maximum(m, jnp.max(sg, axis=0, keepdims=True))
                m_next[sub, kv, 0:1, :] = m

    @pl.when(step % 2 == 0)
    def _():
        stages(s1_ref, m1_ref, s0_ref, m0_ref)

    @pl.when(step % 2 == 1)
    def _():
        stages(s0_ref, m0_ref, s1_ref, m1_ref)


def _wa_attend(kb, pt, meta_kb, meta_pt, band, meta_bias, sink_row):
    b, n, _ = kb.shape
    assert pt.shape[1:] == (n // WA_Q, T_WIDTH, WA_Q)
    nt = n // WA_Q
    assert nt >= 2
    rq, rv = ROW_QB // ATT_WIDTH, ROW_VB // WA_KV_WIDTH

    steps = b * nt

    def scored(s):
        return jnp.divmod(jnp.minimum(s, steps - 1), nt)

    def finished(s):
        return jnp.divmod(jnp.maximum(s - 1, 0), nt)

    def kseg(shift):
        def index(s):
            bb, tile = scored(s)
            return bb, jnp.clip(tile + shift, 0, nt - 1), 0
        return pl.BlockSpec((None, WA_Q, WA_KV_WIDTH), index)

    def vseg(shift):
        def index(s):
            bb, tile = finished(s)
            return bb, jnp.clip(tile + shift, 0, nt - 1), rv, 0
        return pl.BlockSpec((None, None, WA_KV_WIDTH, WA_Q), index)

    def q_index(s):
        bb, tile = scored(s)
        return bb, tile, rq, 0

    def o_index(s):
        bb, tile = finished(s)
        return bb, tile, 0, 0

    def kind_a(s):
        return (jnp.where(scored(s)[1] == 0, 0, 1), 0, 0, 0)

    def kind_b(s):
        return (jnp.where(scored(s)[1] == nt - 1, 2, 1), 0, 0, 0)

    lanes = WA_GROUP * BLOCK
    band_spec = (1, WA_KV_HEADS, 3 * BLOCK, lanes)
    return pl.pallas_call(
        _wa_body,
        grid=(steps + 1,),
        in_specs=[pl.BlockSpec((None, None, ATT_WIDTH, WA_Q), q_index),
                  kseg(-1), kseg(0), kseg(1),
                  vseg(-1), vseg(0), vseg(1),
                  pl.BlockSpec((None, N_META, WA_KV_WIDTH), lambda s: (0, 0, 0)),
                  pl.BlockSpec((None, None, WA_KV_WIDTH, N_META), lambda s: (0, 0, rv, 0)),
                  pl.BlockSpec(band_spec, kind_a), pl.BlockSpec(band_spec, kind_b),
                  pl.BlockSpec((WA_SUB, WA_KV_HEADS, N_META, lanes),
                               lambda s: (scored(s)[1], 0, 0, 0)),
                  pl.BlockSpec((WA_KV_HEADS, 1, lanes), lambda s: (0, 0, 0))],
        out_specs=pl.BlockSpec((None, None, ATT_WIDTH, WA_Q), o_index),
        out_shape=jax.ShapeDtypeStruct((b, nt, ATT_WIDTH, WA_Q), BF16),
        scratch_shapes=[pltpu.VMEM((WA_SUB, WA_KV_HEADS, WA_KEYS, lanes), F32),
                        pltpu.VMEM((WA_SUB, WA_KV_HEADS, 8, lanes), F32)] * 2,
        compiler_params=pltpu.CompilerParams(
            dimension_semantics=("arbitrary",), vmem_limit_bytes=VMEM_LIMIT),
        name="wa_attend",
    )(pt, kb, kb, kb, pt, pt, pt, meta_kb, meta_pt, band, band, meta_bias, sink_row)


def _twice_sigmoid_of_twice(half_x):
    return jnp.tanh(half_x) + 1.0


def _tiles_side_by_side(ref):
    return jnp.concatenate([ref[t] for t in range(ref.shape[0])], axis=1)


def _mix_out_body(x_ref, oa_ref, ob_ref, za_ref, zb_ref, ga_ref, gb_ref,
                  wpa_ref, wpb_ref, wout_ref, fg_ref, y_ref, merged0_ref, merged1_ref):
    step = pl.program_id(0)

    @pl.when(step == 0)
    def _():
        merged1_ref[...] = jnp.zeros_like(merged1_ref)

    def stages(merged_prev, merged_next):
        h = x_ref[...] + lax.dot_general(merged_prev[...], wout_ref[...], _TN,
                                         preferred_element_type=F32)
        ms = jnp.mean(h * h, axis=-1, keepdims=True)
        y_ref[...] = (h * lax.rsqrt(ms + RMS_EPS)) * fg_ref[...]
        za = _tiles_side_by_side(za_ref).astype(F32)
        zb = _tiles_side_by_side(zb_ref).astype(F32)
        ta = (_tiles_side_by_side(oa_ref).astype(F32)
              * (za * _twice_sigmoid_of_twice(za))).astype(BF16)
        tb = (_tiles_side_by_side(ob_ref).astype(F32)
              * (zb * _twice_sigmoid_of_twice(zb))).astype(BF16)
        ya = jnp.dot(wpa_ref[...], ta, preferred_element_type=F32)
        yb = jnp.dot(wpb_ref[...], tb, preferred_element_type=F32)
        merged_next[...] = (
            _twice_sigmoid_of_twice(_tiles_side_by_side(ga_ref).astype(F32)) * ya
            + _twice_sigmoid_of_twice(_tiles_side_by_side(gb_ref).astype(F32)) * yb).astype(BF16)

    @pl.when(step % 2 == 0)
    def _():
        stages(merged1_ref, merged0_ref)

    @pl.when(step % 2 == 1)
    def _():
        stages(merged0_ref, merged1_ref)


def _mix_out(x, oa, ob, pt, wpa_t, wpb_t, wout_half, fg, tm):
    b, n, _ = x.shape
    tile = pt.shape[-1]
    assert n % tm == 0 and tm % tile == 0 and oa.shape == ob.shape == (b, n // tile, ATT_WIDTH, tile)
    steps = b * n // tm
    per_step = tm // tile
    x_tiles = x.reshape(steps, tm, D_MODEL)

    def tiles(a):
        return a.reshape((steps * per_step,) + a.shape[2:])

    def feat(rows, row0):
        return pl.BlockSpec((per_step, rows, tile),
                            lambda s: (jnp.minimum(s, steps - 1), row0 // rows, 0))

    def whole(shape):
        return pl.BlockSpec(shape, lambda s: (0, 0))

    tok = pl.BlockSpec((None, tm, D_MODEL), lambda s: (jnp.maximum(s - 1, 0), 0, 0))
    y = pl.pallas_call(
        _mix_out_body,
        grid=(steps + 1,),
        in_specs=[tok, feat(ATT_WIDTH, 0), feat(ATT_WIDTH, 0),
                  feat(ATT_WIDTH, ROW_ZA), feat(ATT_WIDTH, ROW_ZB),
                  feat(D_MODEL, ROW_GA), feat(D_MODEL, ROW_GB),
                  whole((D_MODEL, ATT_WIDTH)), whole((D_MODEL, ATT_WIDTH)),
                  whole((D_MODEL, D_MODEL)), whole((1, D_MODEL))],
        out_specs=tok,
        out_shape=jax.ShapeDtypeStruct((steps, tm, D_MODEL), F32),
        scratch_shapes=[pltpu.VMEM((D_MODEL, tm), BF16)] * 2,
        compiler_params=pltpu.CompilerParams(
            dimension_semantics=("arbitrary",), vmem_limit_bytes=VMEM_LIMIT),
        name="mix_out",
    )(x_tiles, tiles(oa), tiles(ob), tiles(pt), tiles(pt), tiles(pt), tiles(pt),
      wpa_t, wpb_t, wout_half, fg)
    return y.reshape(b, n, D_MODEL)


def _encode(x, meta, params):
    n = x.shape[1]
    meta_ka, meta_kb, meta_pt = meta
    ka, kb, pt = _norm_proj(x, params["norm_g"], params["w_k"], params["w_t"], PROJ_TM, TOKEN_TILE)
    oa = _na_attend(ka, pt, meta_ka, meta_pt, params["na_bias"])
    band, meta_bias, sink_row = _wa_bias_tables(params["t5_bias"], params["sink"], n // BLOCK)
    ob = _wa_attend(kb, pt, meta_kb, meta_pt, band, meta_bias, sink_row)
    return _mix_out(x, oa, ob, pt, params["w_proj_a_t"], params["w_proj_b_t"], params["w_out_half"],
                    params["final_g"], MIX_TM)


def kernel(x_prompt, x_sample, meta_tokens, norm_g, w_in, na_rpb, sink_logit, w_proj_a, w_proj_b,
           w_out, t5_bias, final_g):
    assert norm_g.shape[0] == 1, "one layer"
    w = w_in[0]
    scale = HEAD_DIM ** -0.5 * LOG2_E
    q_a, k_a, v_a, z_a = (w[:, i * 512:(i + 1) * 512] for i in range(4))
    q_b, k_b, v_b, z_b = w[:, 2048:2560], w[:, 2560:2688], w[:, 2688:2816], w[:, 2816:3328]
    g_a, g_b = w[:, 3328:4352], w[:, 4352:5376]
    w_t = jnp.concatenate([g_a * 0.5, g_b * 0.5, q_a * scale, v_a, z_a * 0.5,
                           q_b * scale, z_b * 0.5, v_b], axis=1)
    params = {
        "norm_g": norm_g[0].reshape(1, D_MODEL).astype(F32),
        "w_k": jnp.concatenate([k_a, k_b], axis=1).astype(BF16),
        "w_t": jnp.transpose(w_t).astype(BF16),
        "na_bias": _na_bias_tables(na_rpb[0]),
        "t5_bias": t5_bias,
        "sink": sink_logit[0],
        "w_proj_a_t": jnp.transpose(w_proj_a[0]).astype(BF16),
        "w_proj_b_t": jnp.transpose(w_proj_b[0]).astype(BF16),
        "w_out_half": (w_out[0] * 0.5).astype(BF16),
        "final_g": final_g.reshape(1, D_MODEL).astype(F32),
    }
    meta = _norm_proj(meta_tokens.astype(F32)[None], params["norm_g"],
                      params["w_k"], params["w_t"], N_META, N_META)
    return (_encode(x_prompt, meta, params), _encode(x_sample, meta, params))
```

```python
import functools
import math

import jax
import jax.numpy as jnp
from jax import lax
from jax.experimental import pallas as pl
from jax.experimental.pallas import tpu as pltpu

F32 = jnp.float32
BF16 = jnp.bfloat16

D_MODEL = 1024
N_META = 16
GRID_W = 64
HEADS = 8
HEAD_DIM = 64
ATT_WIDTH = HEADS * HEAD_DIM
NA_WIN_ROWS = 8
NA_WIN_COLS = 16
WA_KV_HEADS = 2
WA_GROUP = HEADS // WA_KV_HEADS
WA_KV_WIDTH = WA_KV_HEADS * HEAD_DIM
WINDOW = 128
BLOCK = 128
T5_BUCKETS = 32
T5_MAX_DIST = 128
RMS_EPS = 1e-6
NEG_INF = -1e30
LOG2_E = math.log2(math.e)

K_WIDTH = ATT_WIDTH + WA_KV_WIDTH
ROW_GA, ROW_GB, ROW_QA, ROW_VA, ROW_ZA, ROW_QB, ROW_ZB, ROW_VB = (
    0, 1024, 2048, 2560, 3072, 3584, 4096, 4608)
T_WIDTH = 4736

NA_ROWS_PER_STEP = 4
NA_Q = NA_ROWS_PER_STEP * GRID_W
NA_KEY_SEGS = 3
PAIR = 2 * HEAD_DIM
TOKEN_TILE = 256
PROJ_TM = 512
PROJ_ROW_CHUNK = 1184
MIX_TM = 512
VMEM_LIMIT = 52 * 1024 * 1024

_NT = (((1,), (1,)), ((), ()))
_TN = (((0,), (0,)), ((), ()))


def _t5_bucket(rel):
    half = T5_BUCKETS // 2
    exact = half // 2
    ret = jnp.where(rel > 0, half, 0)
    n = jnp.abs(rel)
    nf = jnp.maximum(n, 1).astype(F32)
    large = exact + (jnp.log(nf / exact) / math.log(T5_MAX_DIST / exact)
                     * (half - exact)).astype(jnp.int32)
    large = jnp.minimum(large, half - 1)
    return ret + jnp.where(n < exact, n, large)


def _t5_lookup(tab, rel):
    bucket = _t5_bucket(rel)
    tail = (1,) * bucket.ndim
    hit = bucket[None] == jnp.arange(T5_BUCKETS).reshape((T5_BUCKETS,) + tail)
    vals = jnp.transpose(tab).reshape((tab.shape[1], T5_BUCKETS) + tail)
    return jnp.sum(jnp.where(hit[None], vals, 0.0), axis=1)


def _toeplitz(v, rows, cols):
    length = rows + cols - 1
    assert v.shape[-1] == length
    lead = v.shape[:-1]
    flat = jnp.broadcast_to(v[..., None, :], lead + (rows, length)).reshape(lead + (rows * length,))
    flat = flat[..., rows - 1:rows - 1 + rows * (length - 1)]
    return flat.reshape(lead + (rows, length - 1))[..., :cols]


ONES_ROWS = 16


def _with_ones_rows(v):
    return jnp.concatenate([v, jnp.ones((ONES_ROWS, v.shape[1]), v.dtype)], axis=0)


def _half_padded(q, upper):
    zeros = jnp.zeros_like(q)
    return jnp.concatenate([zeros, q] if upper else [q, zeros], axis=0)


def _norm_proj_body(x_ref, g_ref, wk_ref, wt_ref, oka_ref, okb_ref, ot_ref):
    x = x_ref[...]
    ms = jnp.mean(x * x, axis=-1, keepdims=True)
    u = ((x * lax.rsqrt(ms + RMS_EPS)) * g_ref[...]).astype(BF16)
    keys = jnp.dot(u, wk_ref[...], preferred_element_type=F32).astype(BF16)
    oka_ref[...] = keys[:, :ATT_WIDTH]
    okb_ref[...] = keys[:, ATT_WIDTH:]
    n_tiles, _, tile = ot_ref.shape[1:]
    for r in range(0, T_WIDTH, PROJ_ROW_CHUNK):
        rows = lax.dot_general(wt_ref[r:r + PROJ_ROW_CHUNK, :], u, _NT,
                               preferred_element_type=F32).astype(BF16)
        for t in range(n_tiles):
            ot_ref[0, t, r:r + PROJ_ROW_CHUNK, :] = rows[:, t * tile:(t + 1) * tile]


def _norm_proj(x, g, wk, wt, tm, tile):
    b, n, _ = x.shape
    assert n % tm == 0 and tm % tile == 0
    return pl.pallas_call(
        _norm_proj_body,
        grid=(b, n // tm),
        in_specs=[pl.BlockSpec((None, tm, D_MODEL), lambda bb, i: (bb, i, 0)),
                  pl.BlockSpec((1, D_MODEL), lambda bb, i: (0, 0)),
                  pl.BlockSpec((D_MODEL, K_WIDTH), lambda bb, i: (0, 0)),
                  pl.BlockSpec((T_WIDTH, D_MODEL), lambda bb, i: (0, 0))],
        out_specs=[pl.BlockSpec((None, tm, ATT_WIDTH), lambda bb, i: (bb, i, 0)),
                   pl.BlockSpec((None, tm, WA_KV_WIDTH), lambda bb, i: (bb, i, 0)),
                   pl.BlockSpec((1, tm // tile, T_WIDTH, tile), lambda bb, i: (bb, i, 0, 0))],
        out_shape=[jax.ShapeDtypeStruct((b, n, ATT_WIDTH), BF16),
                   jax.ShapeDtypeStruct((b, n, WA_KV_WIDTH), BF16),
                   jax.ShapeDtypeStruct((b, n // tile, T_WIDTH, tile), BF16)],
        compiler_params=pltpu.CompilerParams(
            dimension_semantics=("arbitrary", "arbitrary"), vmem_limit_bytes=VMEM_LIMIT),
        name="norm_proj",
    )(x, g, wk, wt)


def _na_bias_tables(rpb):
    n_t = NA_KEY_SEGS * NA_ROWS_PER_STEP
    lead = GRID_W - NA_WIN_COLS
    v = jnp.pad(rpb.astype(F32) * LOG2_E, ((0, 0), (0, 0), (lead, lead)))
    cols = _toeplitz(v, GRID_W, GRID_W)
    j = jnp.arange(GRID_W)[:, None]
    c = jnp.arange(GRID_W)[None, :]
    cs = jnp.clip(j - NA_WIN_COLS // 2, 0, GRID_W - NA_WIN_COLS)
    cols = jnp.where((c >= cs) & (c < cs + NA_WIN_COLS), cols, NEG_INF)
    assert NA_WIN_ROWS // 2 - 1 == NA_ROWS_PER_STEP - 1 and cols.shape[1] == NA_ROWS_PER_STEP + n_t - 1
    full = _toeplitz(jnp.transpose(cols, (0, 3, 2, 1)), NA_ROWS_PER_STEP, n_t)
    full = jnp.transpose(full, (0, 4, 1, 3, 2))
    a = jnp.arange(NA_ROWS_PER_STEP)[None, :]
    t = jnp.arange(NA_WIN_T)[:, None]

    def masked(tab, ok):
        return jnp.where(ok[None, :, None, :, None], tab, NEG_INF)

    first8 = (t < NA_WIN_ROWS) & (a >= 0)
    top = jnp.concatenate([full[:, 2 * NA_ROWS_PER_STEP:], full[:, NA_ROWS_PER_STEP:NA_WIN_T]], axis=1)
    tabs = [masked(top, first8),
            masked(full[:, :NA_WIN_T], (t - a >= 0) & (t - a < NA_WIN_ROWS)),
            masked(full[:, :NA_WIN_T], first8)]
    return jnp.stack(tabs).reshape(3, HEADS, NA_WIN_KEYS, NA_Q)


NA_WIN_T = NA_KEY_SEGS * NA_ROWS_PER_STEP - 1
NA_WIN_KEYS = NA_WIN_T * GRID_W
NA_SEG_KEYS = (NA_Q, NA_Q, NA_WIN_KEYS - 2 * NA_Q)
NA_KEYS = NA_WIN_KEYS + N_META


def _na_body(q_ref, kp_ref, kc_ref, kn_ref, vp_ref, vc_ref, vn_ref, km_ref, vm_ref,
             bias_ref, o_ref, s0_ref, m0_ref, s1_ref, m1_ref, *, nblk, steps):
    k_refs = (kp_ref, kc_ref, kn_ref)
    v_refs = (vp_ref, vc_ref, vn_ref)
    n_win = NA_WIN_KEYS
    step = pl.program_id(0)
    blk = lax.rem(jnp.minimum(step, steps - 1), nblk)
    kind = jnp.where(blk == 0, 0, jnp.where(blk == nblk - 1, 2, 1))

    @pl.when(step == 0)
    def _():
        s1_ref[...] = jnp.zeros_like(s1_ref)
        m1_ref[...] = jnp.zeros_like(m1_ref)

    def stages(s_prev, m_prev, s_next, m_next):
        for h in range(HEADS):
            rows = slice(h * HEAD_DIM, (h + 1) * HEAD_DIM)
            m_old = m_prev[h, 0:1, :]
            p_meta = jnp.exp2(s_prev[h, n_win:NA_KEYS, :] - m_old)
            o = jnp.dot(_with_ones_rows(vm_ref[rows, :]), p_meta.astype(BF16),
                        preferred_element_type=F32)
            slab = slice((h // 2) * PAIR, (h // 2 + 1) * PAIR)
            q = _half_padded(q_ref[rows, :], h % 2)
            s_meta = jnp.dot(km_ref[:, slab], q, preferred_element_type=F32)
            s_next[h, n_win:NA_KEYS, :] = s_meta
            m = jnp.max(s_meta, axis=0, keepdims=True)
            for g, (kr, vr) in enumerate(zip(k_refs, v_refs)):
                used = NA_SEG_KEYS[g]
                seg = slice(g * NA_Q, g * NA_Q + used)
                p = jnp.exp2(s_prev[h, seg, :] - m_old).astype(BF16)
                if used < NA_Q:
                    p = jnp.concatenate([p, jnp.zeros((NA_Q - used, NA_Q), BF16)], axis=0)
                o = o + jnp.dot(_with_ones_rows(vr[rows, :]), p, preferred_element_type=F32)
                sg = (jnp.dot(kr[:used, slab], q, preferred_element_type=F32)
                      + bias_ref[kind, h, seg, :])
                s_next[h, seg, :] = sg
                m = jnp.maximum(m, jnp.max(sg, axis=0, keepdims=True))
            o_ref[rows, :] = (o[:HEAD_DIM] / o[HEAD_DIM:HEAD_DIM + 1]).astype(BF16)
            m_next[h, 0:1, :] = m

    @pl.when(step % 2 == 0)
    def _():
        stages(s1_ref, m1_ref, s0_ref, m0_ref)

    @pl.when(step % 2 == 1)
    def _():
        stages(s0_ref, m0_ref, s1_ref, m1_ref)


def _na_attend(ka, pt, meta_ka, meta_pt, bias_tabs):
    b, n, _ = ka.shape
    assert pt.shape[1:] == (n // NA_Q, T_WIDTH, NA_Q)
    nblk = n // NA_Q
    assert nblk >= 3
    rq, rv = ROW_QA // ATT_WIDTH, ROW_VA // ATT_WIDTH

    steps = b * nblk

    def scored(s):
        return jnp.divmod(jnp.minimum(s, steps - 1), nblk)

    def finished(s):
        return jnp.divmod(jnp.maximum(s - 1, 0), nblk)

    def tile_of(blk, seg):
        return (jnp.where(blk == 0, 1, blk - 1), blk, jnp.minimum(blk + 1, nblk - 1))[seg]

    def kseg(seg):
        def index(s):
            bb, blk = scored(s)
            return bb, tile_of(blk, seg), 0
        return pl.BlockSpec((None, NA_Q, ATT_WIDTH), index)

    def vseg(seg):
        def index(s):
            bb, blk = finished(s)
            return bb, tile_of(blk, seg), rv, 0
        return pl.BlockSpec((None, None, ATT_WIDTH, NA_Q), index)

    def q_index(s):
        bb, blk = scored(s)
        return bb, blk, rq, 0

    def o_index(s):
        bb, blk = finished(s)
        return bb, blk, 0, 0

    return pl.pallas_call(
        functools.partial(_na_body, nblk=nblk, steps=steps),
        grid=(steps + 1,),
        in_specs=[pl.BlockSpec((None, None, ATT_WIDTH, NA_Q), q_index),
                  kseg(0), kseg(1), kseg(2),
                  vseg(0), vseg(1), vseg(2),
                  pl.BlockSpec((None, N_META, ATT_WIDTH), lambda s: (0, 0, 0)),
                  pl.BlockSpec((None, None, ATT_WIDTH, N_META), lambda s: (0, 0, rv, 0)),
                  pl.BlockSpec(bias_tabs.shape, lambda s: (0, 0, 0, 0))],
        out_specs=pl.BlockSpec((None, None, ATT_WIDTH, NA_Q), o_index),
        out_shape=jax.ShapeDtypeStruct((b, nblk, ATT_WIDTH, NA_Q), BF16),
        scratch_shapes=[pltpu.VMEM((HEADS, NA_KEYS, NA_Q), F32),
                        pltpu.VMEM((HEADS, 8, NA_Q), F32)] * 2,
        compiler_params=pltpu.CompilerParams(
            dimension_semantics=("arbitrary",), vmem_limit_bytes=VMEM_LIMIT),
        name="na_attend",
    )(pt, ka, ka, ka, pt, pt, pt, meta_ka, meta_pt, bias_tabs)


def _wa_bias_tables(t5_bias, sink, nb):
    tab = t5_bias.astype(F32) * LOG2_E
    qq = jnp.arange(BLOCK)
    kk = jnp.arange(3 * BLOCK)
    rel = kk[None, :] - BLOCK - qq[:, None]
    dist = jnp.arange(-(2 * BLOCK - 1), 2 * BLOCK)
    band = _toeplitz(_t5_lookup(tab, dist), BLOCK, 3 * BLOCK)
    in_window = jnp.abs(rel) <= WINDOW
    seg = (kk // BLOCK)[None, :]
    kinds = (in_window & (seg != 0), in_window, in_window & (seg != 2))
    band = jnp.stack([jnp.where(ok[None], band, NEG_INF) for ok in kinds])
    band = band.reshape(3, WA_KV_HEADS, WA_GROUP, BLOCK, 3 * BLOCK)
    band = jnp.transpose(band, (0, 1, 4, 2, 3)).reshape(3, WA_KV_HEADS, 3 * BLOCK, WA_GROUP * BLOCK)
    n = nb * BLOCK
    per_dist = _t5_lookup(tab, -jnp.arange(n + BLOCK + N_META))
    main = per_dist[:, 1:1 + n].reshape(HEADS, nb, BLOCK)
    tail = per_dist[:, 1 + BLOCK:1 + BLOCK + n].reshape(HEADS, nb, BLOCK)[..., :N_META - 1]
    meta = _toeplitz(jnp.concatenate([main, tail], axis=-1), N_META, BLOCK)
    meta = meta.reshape(WA_KV_HEADS, WA_GROUP, nb, N_META, BLOCK)
    meta = jnp.transpose(meta, (2, 0, 3, 1, 4)).reshape(nb, WA_KV_HEADS, N_META, WA_GROUP * BLOCK)
    sink_row = jnp.broadcast_to((sink.astype(F32) * LOG2_E).reshape(WA_KV_HEADS, WA_GROUP, 1),
                                (WA_KV_HEADS, WA_GROUP, BLOCK)).reshape(WA_KV_HEADS, 1, WA_GROUP * BLOCK)
    return band, meta, sink_row


WA_KEYS = 3 * BLOCK + N_META


WA_TILES = 2
WA_SUB = WA_TILES * TOKEN_TILE // BLOCK


def _wa_body(q_ref, kp_ref, kc_ref, kn_ref, vp_ref, vc_ref, vn_ref, km_ref, vm_ref,
             band_ref, mbias_ref, sink_ref, o_ref, s0_ref, m0_ref, s1_ref, m1_ref, *, npair, steps):
    n_band = 3 * BLOCK
    lo, hi = slice(0, BLOCK), slice(BLOCK, TOKEN_TILE)
    step = pl.program_id(0)
    pair = lax.rem(jnp.minimum(step, steps - 1), npair)
    kinds = ([jnp.where(pair == 0, 0, 1)] + [1] * (WA_SUB - 2)
             + [jnp.where(pair == npair - 1, 2, 1)])

    def key_block(j):
        if j == 0:
            return kp_ref[hi, :]
        if j == WA_SUB + 1:
            return kn_ref[lo, :]
        return kc_ref[(j - 1) * BLOCK:j * BLOCK, :]

    def val_block(j, vrows):
        if j == 0:
            return vp_ref[vrows, hi]
        if j == WA_SUB + 1:
            return vn_ref[vrows, lo]
        return vc_ref[(j - 1) // 2, vrows, (lo, hi)[(j - 1) % 2]]

    @pl.when(step == 0)
    def _():
        s1_ref[...] = jnp.zeros_like(s1_ref)
        m1_ref[...] = jnp.zeros_like(m1_ref)

    def stages(s_prev, m_prev, s_next, m_next):
        for sub in range(WA_SUB):
            tile, cols = sub // 2, (lo, hi)[sub % 2]
            for kv in range(WA_KV_HEADS):
                vrows = slice(kv * HEAD_DIM, (kv + 1) * HEAD_DIM)
                m = m_prev[sub, kv, 0:1, :]
                p_meta = jnp.exp2(s_prev[sub, kv, n_band:WA_KEYS, :] - m)
                o = jnp.dot(_with_ones_rows(vm_ref[vrows, :]), p_meta.astype(BF16),
                            preferred_element_type=F32)
                for g in range(3):
                    p = jnp.exp2(s_prev[sub, kv, g * BLOCK:(g + 1) * BLOCK, :] - m)
                    o = o + jnp.dot(_with_ones_rows(val_block(sub + g, vrows)), p.astype(BF16),
                                    preferred_element_type=F32)
                l = o[HEAD_DIM:HEAD_DIM + 1] + jnp.exp2(sink_ref[kv] - m)
                o = (o[:HEAD_DIM] / l).astype(BF16)
                for g in range(WA_GROUP):
                    h = kv * WA_GROUP + g
                    o_ref[tile, h * HEAD_DIM:(h + 1) * HEAD_DIM, cols] = (
                        o[:, g * BLOCK:(g + 1) * BLOCK])
                q4 = jnp.concatenate(
                    [q_ref[tile, (kv * WA_GROUP + g) * HEAD_DIM:(kv * WA_GROUP + g + 1) * HEAD_DIM,
                           cols] for g in range(WA_GROUP)], axis=1)
                q4 = _half_padded(q4, kv)
                s_meta = (jnp.dot(km_ref[...], q4, preferred_element_type=F32)
                          + mbias_ref[sub, kv])
                s_next[sub, kv, n_band:WA_KEYS, :] = s_meta
                m = jnp.maximum(jnp.max(s_meta, axis=0, keepdims=True), sink_ref[kv])
                for g in range(3):
                    sg = (jnp.dot(key_block(sub + g), q4, preferred_element_type=F32)
                          + band_ref[kinds[sub], kv, g * BLOCK:(g + 1) * BLOCK, :])
                    s_next[sub, kv, g * BLOCK:(g + 1) * BLOCK, :] = sg
                    m = jnp.maximum(m, jnp.max(sg, axis=0, keepdims=True))
                m_next[sub, kv, 0:1, :] = m

    @pl.when(step % 2 == 0)
    def _():
        stages(s1_ref, m1_ref, s0_ref, m0_ref)

    @pl.when(step % 2 == 1)
    def _():
        stages(s0_ref, m0_ref, s1_ref, m1_ref)


def _wa_attend(kb, pt, meta_kb, meta_pt, band, meta_bias, sink_row):
    b, n, _ = kb.shape
    nt = n // TOKEN_TILE
    assert pt.shape[1:] == (nt, T_WIDTH, TOKEN_TILE) and nt % WA_TILES == 0
    npair = nt // WA_TILES
    rq, rv = ROW_QB // ATT_WIDTH, ROW_VB // WA_KV_WIDTH

    steps = b * npair

    def scored(s):
        return jnp.divmod(jnp.minimum(s, steps - 1), npair)

    def finished(s):
        return jnp.divmod(jnp.maximum(s - 1, 0), npair)

    def before(pair):
        return jnp.maximum(pair * WA_TILES - 1, 0)

    def after(pair):
        return jnp.minimum((pair + 1) * WA_TILES, nt - 1)

    def k_spec(tiles, tile_of):
        def index(s):
            bb, pair = scored(s)
            return bb, tile_of(pair), 0
        return pl.BlockSpec((None, tiles * TOKEN_TILE, WA_KV_WIDTH), index)

    def v_spec(tile_of):
        def index(s):
            bb, pair = finished(s)
            return bb, tile_of(pair), rv, 0
        return pl.BlockSpec((None, None, WA_KV_WIDTH, TOKEN_TILE), index)

    def pair_spec(rows, which, row_block):
        def index(s):
            bb, pair = which(s)
            return bb, pair, row_block, 0
        return pl.BlockSpec((None, WA_TILES, rows, TOKEN_TILE), index)

    lanes = WA_GROUP * BLOCK
    return pl.pallas_call(
        functools.partial(_wa_body, npair=npair, steps=steps),
        grid=(steps + 1,),
        in_specs=[pair_spec(ATT_WIDTH, scored, rq),
                  k_spec(1, before), k_spec(WA_TILES, lambda pair: pair), k_spec(1, after),
                  v_spec(before), pair_spec(WA_KV_WIDTH, finished, rv), v_spec(after),
                  pl.BlockSpec((None, N_META, WA_KV_WIDTH), lambda s: (0, 0, 0)),
                  pl.BlockSpec((None, None, WA_KV_WIDTH, N_META), lambda s: (0, 0, rv, 0)),
                  pl.BlockSpec(band.shape, lambda s: (0, 0, 0, 0)),
                  pl.BlockSpec((WA_SUB, WA_KV_HEADS, N_META, lanes),
                               lambda s: (scored(s)[1], 0, 0, 0)),
                  pl.BlockSpec((WA_KV_HEADS, 1, lanes), lambda s: (0, 0, 0))],
        out_specs=pair_spec(ATT_WIDTH, finished, 0),
        out_shape=jax.ShapeDtypeStruct((b, nt, ATT_WIDTH, TOKEN_TILE), BF16),
        scratch_shapes=[pltpu.VMEM((WA_SUB, WA_KV_HEADS, WA_KEYS, lanes), F32),
                        pltpu.VMEM((WA_SUB, WA_KV_HEADS, 8, lanes), F32)] * 2,
        compiler_params=pltpu.CompilerParams(
            dimension_semantics=("arbitrary",), vmem_limit_bytes=VMEM_LIMIT),
        name="wa_attend",
    )(pt, kb, kb, kb, pt, pt, pt, meta_kb, meta_pt, band, meta_bias, sink_row)


def _twice_sigmoid_of_twice(half_x):
    return jnp.tanh(half_x) + 1.0


def _tiles_side_by_side(ref):
    return jnp.concatenate([ref[t] for t in range(ref.shape[0])], axis=1)


def _mix_out_body(x_ref, oa_ref, ob_ref, za_ref, zb_ref, ga_ref, gb_ref,
                  wpa_ref, wpb_ref, wout_ref, fg_ref, y_ref, merged0_ref, merged1_ref):
    step = pl.program_id(0)

    @pl.when(step == 0)
    def _():
        merged1_ref[...] = jnp.zeros_like(merged1_ref)

    def stages(merged_prev, merged_next):
        h = x_ref[...] + lax.dot_general(merged_prev[...], wout_ref[...], _TN,
                                         preferred_element_type=F32)
        ms = jnp.mean(h * h, axis=-1, keepdims=True)
        y_ref[...] = (h * lax.rsqrt(ms + RMS_EPS)) * fg_ref[...]
        za = _tiles_side_by_side(za_ref).astype(F32)
        zb = _tiles_side_by_side(zb_ref).astype(F32)
        ta = (_tiles_side_by_side(oa_ref).astype(F32)
              * (za * _twice_sigmoid_of_twice(za))).astype(BF16)
        tb = (_tiles_side_by_side(ob_ref).astype(F32)
              * (zb * _twice_sigmoid_of_twice(zb))).astype(BF16)
        ya = jnp.dot(wpa_ref[...], ta, preferred_element_type=F32)
        yb = jnp.dot(wpb_ref[...], tb, preferred_element_type=F32)
        merged_next[...] = (
            _twice_sigmoid_of_twice(_tiles_side_by_side(ga_ref).astype(F32)) * ya
            + _twice_sigmoid_of_twice(_tiles_side_by_side(gb_ref).astype(F32)) * yb).astype(BF16)

    @pl.when(step % 2 == 0)
    def _():
        stages(merged1_ref, merged0_ref)

    @pl.when(step % 2 == 1)
    def _():
        stages(merged0_ref, merged1_ref)


def _mix_out(x, oa, ob, pt, wpa_t, wpb_t, wout_half, fg, tm):
    b, n, _ = x.shape
    tile = pt.shape[-1]
    assert n % tm == 0 and tm % tile == 0 and oa.shape == ob.shape == (b, n // tile, ATT_WIDTH, tile)
    steps = b * n // tm
    per_step = tm // tile
    x_tiles = x.reshape(steps, tm, D_MODEL)

    def tiles(a):
        return a.reshape((steps * per_step,) + a.shape[2:])

    def feat(rows, row0):
        return pl.BlockSpec((per_step, rows, tile),
                            lambda s: (jnp.minimum(s, steps - 1), row0 // rows, 0))

    def whole(shape):
        return pl.BlockSpec(shape, lambda s: (0, 0))

    tok = pl.BlockSpec((None, tm, D_MODEL), lambda s: (jnp.maximum(s - 1, 0), 0, 0))
    y = pl.pallas_call(
        _mix_out_body,
        grid=(steps + 1,),
        in_specs=[tok, feat(ATT_WIDTH, 0), feat(ATT_WIDTH, 0),
                  feat(ATT_WIDTH, ROW_ZA), feat(ATT_WIDTH, ROW_ZB),
                  feat(D_MODEL, ROW_GA), feat(D_MODEL, ROW_GB),
                  whole((D_MODEL, ATT_WIDTH)), whole((D_MODEL, ATT_WIDTH)),
                  whole((D_MODEL, D_MODEL)), whole((1, D_MODEL))],
        out_specs=tok,
        out_shape=jax.ShapeDtypeStruct((steps, tm, D_MODEL), F32),
        scratch_shapes=[pltpu.VMEM((D_MODEL, tm), BF16)] * 2,
        compiler_params=pltpu.CompilerParams(
            dimension_semantics=("arbitrary",), vmem_limit_bytes=VMEM_LIMIT),
        name="mix_out",
    )(x_tiles, tiles(oa), tiles(ob), tiles(pt), tiles(pt), tiles(pt), tiles(pt),
      wpa_t, wpb_t, wout_half, fg)
    return y.reshape(b, n, D_MODEL)


def _encode(x, meta, params):
    n = x.shape[1]
    meta_ka, meta_kb, meta_pt = meta
    ka, kb, pt = _norm_proj(x, params["norm_g"], params["w_k"], params["w_t"], PROJ_TM, TOKEN_TILE)
    oa = _na_attend(ka, pt, meta_ka, meta_pt, params["na_bias"])
    band, meta_bias, sink_row = params["wa_bias"]
    assert meta_bias.shape[0] >= n // BLOCK
    ob = _wa_attend(kb, pt, meta_kb, meta_pt, band, meta_bias, sink_row)
    return _mix_out(x, oa, ob, pt, params["w_proj_a_t"], params["w_proj_b_t"], params["w_out_half"],
                    params["final_g"], MIX_TM)


def kernel(x_prompt, x_sample, meta_tokens, norm_g, w_in, na_rpb, sink_logit, w_proj_a, w_proj_b,
           w_out, t5_bias, final_g):
    assert norm_g.shape[0] == 1, "one layer"
    w = w_in[0]
    scale = HEAD_DIM ** -0.5 * LOG2_E
    col = {"qA": (0, 512, scale), "kA": (512, 1024, 1.0), "vA": (1024, 1536, 1.0),
           "zA": (1536, 2048, 0.5), "qB": (2048, 2560, scale), "kB": (2560, 2688, 1.0),
           "vB": (2688, 2816, 1.0), "zB": (2816, 3328, 0.5), "gA": (3328, 4352, 0.5),
           "gB": (4352, 5376, 0.5)}
    w_rows = jnp.transpose(w)
    feature_major = ("gA", "gB", "qA", "vA", "zA", "qB", "zB", "vB")
    w_t = jnp.concatenate([w_rows[col[c][0]:col[c][1]] * col[c][2] for c in feature_major], axis=0)
    max_blocks = max(x_prompt.shape[1], x_sample.shape[1]) // BLOCK
    params = {
        "norm_g": norm_g[0].reshape(1, D_MODEL).astype(F32),
        "w_k": jnp.concatenate([w[:, 512:1024], w[:, 2560:2688]], axis=1).astype(BF16),
        "w_t": w_t.astype(BF16),
        "na_bias": _na_bias_tables(na_rpb[0]),
        "wa_bias": _wa_bias_tables(t5_bias, sink_logit[0], max_blocks),
        "w_proj_a_t": jnp.transpose(w_proj_a[0]).astype(BF16),
        "w_proj_b_t": jnp.transpose(w_proj_b[0]).astype(BF16),
        "w_out_half": (w_out[0] * 0.5).astype(BF16),
        "final_g": final_g.reshape(1, D_MODEL).astype(F32),
    }
    meta = _norm_proj(meta_tokens.astype(F32)[None], params["norm_g"],
                      params["w_k"], params["w_t"], N_META, N_META)
    return (_encode(x_prompt, meta, params), _encode(x_sample, meta, params))
```

```python
import functools
import math

import jax
import jax.numpy as jnp
from jax import lax
from jax.experimental import pallas as pl
from jax.experimental.pallas import tpu as pltpu

F32 = jnp.float32
BF16 = jnp.bfloat16

D_MODEL = 1024
N_META = 16
GRID_W = 64
HEADS = 8
HEAD_DIM = 64
ATT_WIDTH = HEADS * HEAD_DIM
NA_WIN_ROWS = 8
NA_WIN_COLS = 16
WA_KV_HEADS = 2
WA_GROUP = HEADS // WA_KV_HEADS
WA_KV_WIDTH = WA_KV_HEADS * HEAD_DIM
WINDOW = 128
BLOCK = 128
T5_BUCKETS = 32
T5_MAX_DIST = 128
RMS_EPS = 1e-6
NEG_INF = -1e30
LOG2_E = math.log2(math.e)

K_WIDTH = ATT_WIDTH + WA_KV_WIDTH
ROW_GA, ROW_GB, ROW_QA, ROW_VA, ROW_ZA, ROW_QB, ROW_ZB, ROW_VB = (
    0, 1024, 2048, 2560, 3072, 3584, 4096, 4608)
T_WIDTH = 4736

NA_ROWS_PER_STEP = 4
NA_Q = NA_ROWS_PER_STEP * GRID_W
NA_KEY_SEGS = 3
PAIR = 2 * HEAD_DIM
TOKEN_TILE = 256
PROJ_TM = 512
PROJ_ROW_CHUNK = 1184
MIX_TM = 512
VMEM_LIMIT = 52 * 1024 * 1024

_NT = (((1,), (1,)), ((), ()))
_TN = (((0,), (0,)), ((), ()))


def _t5_bucket(rel):
    half = T5_BUCKETS // 2
    exact = half // 2
    ret = jnp.where(rel > 0, half, 0)
    n = jnp.abs(rel)
    nf = jnp.maximum(n, 1).astype(F32)
    large = exact + (jnp.log(nf / exact) / math.log(T5_MAX_DIST / exact)
                     * (half - exact)).astype(jnp.int32)
    large = jnp.minimum(large, half - 1)
    return ret + jnp.where(n < exact, n, large)


def _t5_lookup(tab, rel):
    bucket = _t5_bucket(rel)
    tail = (1,) * bucket.ndim
    hit = bucket[None] == jnp.arange(T5_BUCKETS).reshape((T5_BUCKETS,) + tail)
    vals = jnp.transpose(tab).reshape((tab.shape[1], T5_BUCKETS) + tail)
    return jnp.sum(jnp.where(hit[None], vals, 0.0), axis=1)


def _toeplitz(v, rows, cols):
    length = rows + cols - 1
    assert v.shape[-1] == length
    lead = v.shape[:-1]
    flat = jnp.broadcast_to(v[..., None, :], lead + (rows, length)).reshape(lead + (rows * length,))
    flat = flat[..., rows - 1:rows - 1 + rows * (length - 1)]
    return flat.reshape(lead + (rows, length - 1))[..., :cols]


ONES_ROWS = 16


def _with_ones_rows(v):
    return jnp.concatenate([v, jnp.ones((ONES_ROWS, v.shape[1]), v.dtype)], axis=0)


def _half_padded(q, upper):
    zeros = jnp.zeros_like(q)
    return jnp.concatenate([zeros, q] if upper else [q, zeros], axis=0)


def _norm_proj_body(x_ref, g_ref, wk_ref, wt_ref, oka_ref, okb_ref, ot_ref):
    x = x_ref[...]
    ms = jnp.mean(x * x, axis=-1, keepdims=True)
    u = ((x * lax.rsqrt(ms + RMS_EPS)) * g_ref[...]).astype(BF16)
    keys = jnp.dot(u, wk_ref[...], preferred_element_type=F32).astype(BF16)
    oka_ref[...] = keys[:, :ATT_WIDTH]
    okb_ref[...] = keys[:, ATT_WIDTH:]
    n_tiles, _, tile = ot_ref.shape[1:]
    for r in range(0, T_WIDTH, PROJ_ROW_CHUNK):
        rows = lax.dot_general(wt_ref[r:r + PROJ_ROW_CHUNK, :], u, _NT,
                               preferred_element_type=F32).astype(BF16)
        for t in range(n_tiles):
            ot_ref[0, t, r:r + PROJ_ROW_CHUNK, :] = rows[:, t * tile:(t + 1) * tile]


def _norm_proj(x, g, wk, wt, tm, tile):
    b, n, _ = x.shape
    assert n % tm == 0 and tm % tile == 0
    return pl.pallas_call(
        _norm_proj_body,
        grid=(b, n // tm),
        in_specs=[pl.BlockSpec((None, tm, D_MODEL), lambda bb, i: (bb, i, 0)),
                  pl.BlockSpec((1, D_MODEL), lambda bb, i: (0, 0)),
                  pl.BlockSpec((D_MODEL, K_WIDTH), lambda bb, i: (0, 0)),
                  pl.BlockSpec((T_WIDTH, D_MODEL), lambda bb, i: (0, 0))],
        out_specs=[pl.BlockSpec((None, tm, ATT_WIDTH), lambda bb, i: (bb, i, 0)),
                   pl.BlockSpec((None, tm, WA_KV_WIDTH), lambda bb, i: (bb, i, 0)),
                   pl.BlockSpec((1, tm // tile, T_WIDTH, tile), lambda bb, i: (bb, i, 0, 0))],
        out_shape=[jax.ShapeDtypeStruct((b, n, ATT_WIDTH), BF16),
                   jax.ShapeDtypeStruct((b, n, WA_KV_WIDTH), BF16),
                   jax.ShapeDtypeStruct((b, n // tile, T_WIDTH, tile), BF16)],
        compiler_params=pltpu.CompilerParams(
            dimension_semantics=("arbitrary", "arbitrary"), vmem_limit_bytes=VMEM_LIMIT),
        name="norm_proj",
    )(x, g, wk, wt)


def _na_bias_tables(rpb):
    n_t = NA_KEY_SEGS * NA_ROWS_PER_STEP
    lead = GRID_W - NA_WIN_COLS
    v = jnp.pad(rpb.astype(F32) * LOG2_E, ((0, 0), (0, 0), (lead, lead)))
    cols = _toeplitz(v, GRID_W, GRID_W)
    j = jnp.arange(GRID_W)[:, None]
    c = jnp.arange(GRID_W)[None, :]
    cs = jnp.clip(j - NA_WIN_COLS // 2, 0, GRID_W - NA_WIN_COLS)
    cols = jnp.where((c >= cs) & (c < cs + NA_WIN_COLS), cols, NEG_INF)
    cols_t = jnp.swapaxes(cols, -1, -2)
    top_off = NA_WIN_ROWS // 2 - 1
    assert top_off - (NA_ROWS_PER_STEP - 1) >= 0 and top_off + n_t <= cols.shape[1]
    full = jnp.stack([cols_t[:, top_off - a:top_off - a + n_t]
                      for a in range(NA_ROWS_PER_STEP)], axis=3)
    a = jnp.arange(NA_ROWS_PER_STEP)[None, :]
    t = jnp.arange(NA_WIN_T)[:, None]

    def masked(tab, ok):
        return jnp.where(ok[None, :, None, :, None], tab, NEG_INF)

    first8 = (t < NA_WIN_ROWS) & (a >= 0)
    top = jnp.concatenate([full[:, 2 * NA_ROWS_PER_STEP:], full[:, NA_ROWS_PER_STEP:NA_WIN_T]], axis=1)
    tabs = [masked(top, first8),
            masked(full[:, :NA_WIN_T], (t - a >= 0) & (t - a < NA_WIN_ROWS)),
            masked(full[:, :NA_WIN_T], first8)]
    return jnp.stack(tabs).reshape(3, HEADS, NA_WIN_KEYS, NA_Q)


NA_WIN_T = NA_KEY_SEGS * NA_ROWS_PER_STEP - 1
NA_WIN_KEYS = NA_WIN_T * GRID_W
NA_SEG_KEYS = (NA_Q, NA_Q, NA_WIN_KEYS - 2 * NA_Q)
NA_KEYS = NA_WIN_KEYS + N_META


NA_TILES = 2


def _na_body(q_ref, kb_ref, kc_ref, ka_ref, vb_ref, vc_ref, va_ref, km_ref, vm_ref,
             bias_ref, o_ref, s_ref, m_ref, *, npair, steps):
    n_win = NA_WIN_KEYS
    step = pl.program_id(0)
    pair = lax.rem(jnp.minimum(step, steps - 1), npair)
    kinds = (jnp.where(pair == 0, 0, 1), jnp.where(pair == npair - 1, 2, 1))

    def key_seg(t, g, used, slab):
        if t + g == 0:
            return kb_ref[:used, slab]
        if t + g == NA_TILES + 1:
            return ka_ref[:used, slab]
        return kc_ref[(t + g - 1) * NA_Q:(t + g - 1) * NA_Q + used, slab]

    def val_seg(t, g, rows):
        if t + g == 0:
            return vb_ref[rows, :]
        if t + g == NA_TILES + 1:
            return va_ref[rows, :]
        return vc_ref[t + g - 1, rows, :]

    @pl.when(step == 0)
    def _():
        s_ref[...] = jnp.zeros_like(s_ref)
        m_ref[...] = jnp.zeros_like(m_ref)

    for t in range(NA_TILES):
        for h in range(HEADS):
            rows = slice(h * HEAD_DIM, (h + 1) * HEAD_DIM)
            m_old = m_ref[t, h, 0:1, :]
            p_meta = jnp.exp2(s_ref[t, h, n_win:NA_KEYS, :] - m_old)
            o = jnp.dot(_with_ones_rows(vm_ref[rows, :]), p_meta.astype(BF16),
                        preferred_element_type=F32)
            slab = slice((h // 2) * PAIR, (h // 2 + 1) * PAIR)
            q = _half_padded(q_ref[t, rows, :], h % 2)
            s_meta = jnp.dot(km_ref[:, slab], q, preferred_element_type=F32)
            m = jnp.max(s_meta, axis=0, keepdims=True)
            for g in range(NA_KEY_SEGS):
                used = NA_SEG_KEYS[g]
                seg = slice(g * NA_Q, g * NA_Q + used)
                p = jnp.exp2(s_ref[t, h, seg, :] - m_old).astype(BF16)
                if used < NA_Q:
                    p = jnp.concatenate([p, jnp.zeros((NA_Q - used, NA_Q), BF16)], axis=0)
                o = o + jnp.dot(_with_ones_rows(val_seg(t, g, rows)), p,
                                preferred_element_type=F32)
                sg = (jnp.dot(key_seg(t, g, used, slab), q, preferred_element_type=F32)
                      + bias_ref[kinds[t], h, seg, :])
                s_ref[t, h, seg, :] = sg
                m = jnp.maximum(m, jnp.max(sg, axis=0, keepdims=True))
            s_ref[t, h, n_win:NA_KEYS, :] = s_meta
            o_ref[t, rows, :] = (o[:HEAD_DIM] / o[HEAD_DIM:HEAD_DIM + 1]).astype(BF16)
            m_ref[t, h, 0:1, :] = m


def _na_attend(ka, pt, meta_ka, meta_pt, bias_tabs):
    b, n, _ = ka.shape
    nblk = n // NA_Q
    assert pt.shape[1:] == (nblk, T_WIDTH, NA_Q) and nblk % NA_TILES == 0
    assert nblk >= 4
    npair = nblk // NA_TILES
    rq, rv = ROW_QA // ATT_WIDTH, ROW_VA // ATT_WIDTH

    steps = b * npair

    def scored(s):
        return jnp.divmod(jnp.minimum(s, steps - 1), npair)

    def finished(s):
        return jnp.divmod(jnp.maximum(s - 1, 0), npair)

    def before(pair):
        return jnp.where(pair == 0, 1, pair * NA_TILES - 1)

    def after(pair):
        return jnp.minimum((pair + 1) * NA_TILES, nblk - 1)

    def k_spec(tiles, tile_of):
        def index(s):
            bb, pair = scored(s)
            return bb, tile_of(pair), 0
        return pl.BlockSpec((None, tiles * NA_Q, ATT_WIDTH), index)

    def v_spec(tile_of):
        def index(s):
            bb, pair = finished(s)
            return bb, tile_of(pair), rv, 0
        return pl.BlockSpec((None, None, ATT_WIDTH, NA_Q), index)

    def pair_spec(which, row_block):
        def index(s):
            bb, pair = which(s)
            return bb, pair, row_block, 0
        return pl.BlockSpec((None, NA_TILES, ATT_WIDTH, NA_Q), index)

    return pl.pallas_call(
        functools.partial(_na_body, npair=npair, steps=steps),
        grid=(steps + 1,),
        in_specs=[pair_spec(scored, rq),
                  k_spec(1, before), k_spec(NA_TILES, lambda pair: pair), k_spec(1, after),
                  v_spec(before), pair_spec(finished, rv), v_spec(after),
                  pl.BlockSpec((None, N_META, ATT_WIDTH), lambda s: (0, 0, 0)),
                  pl.BlockSpec((None, None, ATT_WIDTH, N_META), lambda s: (0, 0, rv, 0)),
                  pl.BlockSpec(bias_tabs.shape, lambda s: (0, 0, 0, 0))],
        out_specs=pair_spec(finished, 0),
        out_shape=jax.ShapeDtypeStruct((b, nblk, ATT_WIDTH, NA_Q), BF16),
        scratch_shapes=[pltpu.VMEM((NA_TILES, HEADS, NA_KEYS, NA_Q), F32),
                        pltpu.VMEM((NA_TILES, HEADS, 8, NA_Q), F32)],
        compiler_params=pltpu.CompilerParams(
            dimension_semantics=("arbitrary",), vmem_limit_bytes=VMEM_LIMIT),
        name="na_attend",
    )(pt, ka, ka, ka, pt, pt, pt, meta_ka, meta_pt, bias_tabs)


def _wa_bias_tables(t5_bias, sink, nb):
    tab = t5_bias.astype(F32) * LOG2_E
    qq = jnp.arange(BLOCK)
    kk = jnp.arange(3 * BLOCK)
    rel = kk[None, :] - BLOCK - qq[:, None]
    dist = jnp.arange(-(2 * BLOCK - 1), 2 * BLOCK)
    band = _toeplitz(_t5_lookup(tab, dist), BLOCK, 3 * BLOCK)
    in_window = jnp.abs(rel) <= WINDOW
    seg = (kk // BLOCK)[None, :]
    kinds = (in_window & (seg != 0), in_window, in_window & (seg != 2))
    band = jnp.stack([jnp.where(ok[None], band, NEG_INF) for ok in kinds])
    band = band.reshape(3, WA_KV_HEADS, WA_GROUP, BLOCK, 3 * BLOCK)
    band = jnp.transpose(band, (0, 1, 4, 2, 3)).reshape(3, WA_KV_HEADS, 3 * BLOCK, WA_GROUP * BLOCK)
    n = nb * BLOCK
    per_dist = _t5_lookup(tab, -jnp.arange(n + BLOCK + N_META))
    main = per_dist[:, 1:1 + n].reshape(HEADS, nb, BLOCK)
    tail = per_dist[:, 1 + BLOCK:1 + BLOCK + n].reshape(HEADS, nb, BLOCK)[..., :N_META - 1]
    meta = _toeplitz(jnp.concatenate([main, tail], axis=-1), N_META, BLOCK)
    meta = meta.reshape(WA_KV_HEADS, WA_GROUP, nb, N_META, BLOCK)
    meta = jnp.transpose(meta, (2, 0, 3, 1, 4)).reshape(nb, WA_KV_HEADS, N_META, WA_GROUP * BLOCK)
    sink_row = jnp.broadcast_to((sink.astype(F32) * LOG2_E).reshape(WA_KV_HEADS, WA_GROUP, 1),
                                (WA_KV_HEADS, WA_GROUP, BLOCK)).reshape(WA_KV_HEADS, 1, WA_GROUP * BLOCK)
    return band, meta, sink_row


WA_KEYS = 3 * BLOCK + N_META


WA_TILES = 2
WA_SUB = WA_TILES * TOKEN_TILE // BLOCK


def _wa_body(q_ref, kp_ref, kc_ref, kn_ref, vp_ref, vc_ref, vn_ref, km_ref, vm_ref,
             band_ref, mbias_ref, sink_ref, o_ref, s0_ref, m0_ref, s1_ref, m1_ref, *, npair, steps):
    n_band = 3 * BLOCK
    lo, hi = slice(0, BLOCK), slice(BLOCK, TOKEN_TILE)
    step = pl.program_id(0)
    pair = lax.rem(jnp.minimum(step, steps - 1), npair)
    kinds = ([jnp.where(pair == 0, 0, 1)] + [1] * (WA_SUB - 2)
             + [jnp.where(pair == npair - 1, 2, 1)])

    def key_block(j):
        if j == 0:
            return kp_ref[hi, :]
        if j == WA_SUB + 1:
            return kn_ref[lo, :]
        return kc_ref[(j - 1) * BLOCK:j * BLOCK, :]

    def val_block(j, vrows):
        if j == 0:
            return vp_ref[vrows, hi]
        if j == WA_SUB + 1:
            return vn_ref[vrows, lo]
        return vc_ref[(j - 1) // 2, vrows, (lo, hi)[(j - 1) % 2]]

    @pl.when(step == 0)
    def _():
        s1_ref[...] = jnp.zeros_like(s1_ref)
        m1_ref[...] = jnp.zeros_like(m1_ref)

    def stages(s_prev, m_prev, s_next, m_next):
        for sub in range(WA_SUB):
            tile, cols = sub // 2, (lo, hi)[sub % 2]
            for kv in range(WA_KV_HEADS):
                vrows = slice(kv * HEAD_DIM, (kv + 1) * HEAD_DIM)
                m = m_prev[sub, kv, 0:1, :]
                p_meta = jnp.exp2(s_prev[sub, kv, n_band:WA_KEYS, :] - m)
                o = jnp.dot(_with_ones_rows(vm_ref[vrows, :]), p_meta.astype(BF16),
                            preferred_element_type=F32)
                for g in range(3):
                    p = jnp.exp2(s_prev[sub, kv, g * BLOCK:(g + 1) * BLOCK, :] - m)
                    o = o + jnp.dot(_with_ones_rows(val_block(sub + g, vrows)), p.astype(BF16),
                                    preferred_element_type=F32)
                l = o[HEAD_DIM:HEAD_DIM + 1] + jnp.exp2(sink_ref[kv] - m)
                o = (o[:HEAD_DIM] / l).astype(BF16)
                for g in range(WA_GROUP):
                    h = kv * WA_GROUP + g
                    o_ref[tile, h * HEAD_DIM:(h + 1) * HEAD_DIM, cols] = (
                        o[:, g * BLOCK:(g + 1) * BLOCK])
                q4 = jnp.concatenate(
                    [q_ref[tile, (kv * WA_GROUP + g) * HEAD_DIM:(kv * WA_GROUP + g + 1) * HEAD_DIM,
                           cols] for g in range(WA_GROUP)], axis=1)
                q4 = _half_padded(q4, kv)
                s_meta = (jnp.dot(km_ref[...], q4, preferred_element_type=F32)
                          + mbias_ref[sub, kv])
                s_next[sub, kv, n_band:WA_KEYS, :] = s_meta
                m = jnp.maximum(jnp.max(s_meta, axis=0, keepdims=True), sink_ref[kv])
                for g in range(3):
                    sg = (jnp.dot(key_block(sub + g), q4, preferred_element_type=F32)
                          + band_ref[kinds[sub], kv, g * BLOCK:(g + 1) * BLOCK, :])
                    s_next[sub, kv, g * BLOCK:(g + 1) * BLOCK, :] = sg
                    m = jnp.maximum(m, jnp.max(sg, axis=0, keepdims=True))
                m_next[sub, kv, 0:1, :] = m

    @pl.when(step % 2 == 0)
    def _():
        stages(s1_ref, m1_ref, s0_ref, m0_ref)

    @pl.when(step % 2 == 1)
    def _():
        stages(s0_ref, m0_ref, s1_ref, m1_ref)


def _wa_attend(kb, pt, meta_kb, meta_pt, band, meta_bias, sink_row):
    b, n, _ = kb.shape
    nt = n // TOKEN_TILE
    assert pt.shape[1:] == (nt, T_WIDTH, TOKEN_TILE) and nt % WA_TILES == 0
    npair = nt // WA_TILES
    rq, rv = ROW_QB // ATT_WIDTH, ROW_VB // WA_KV_WIDTH

    steps = b * npair

    def scored(s):
        return jnp.divmod(jnp.minimum(s, steps - 1), npair)

    def finished(s):
        return jnp.divmod(jnp.maximum(s - 1, 0), npair)

    def before(pair):
        return jnp.maximum(pair * WA_TILES - 1, 0)

    def after(pair):
        return jnp.minimum((pair + 1) * WA_TILES, nt - 1)

    def k_spec(tiles, tile_of):
        def index(s):
            bb, pair = scored(s)
            return bb, tile_of(pair), 0
        return pl.BlockSpec((None, tiles * TOKEN_TILE, WA_KV_WIDTH), index)

    def v_spec(tile_of):
        def index(s):
            bb, pair = finished(s)
            return bb, tile_of(pair), rv, 0
        return pl.BlockSpec((None, None, WA_KV_WIDTH, TOKEN_TILE), index)

    def pair_spec(rows, which, row_block):
        def index(s):
            bb, pair = which(s)
            return bb, pair, row_block, 0
        return pl.BlockSpec((None, WA_TILES, rows, TOKEN_TILE), index)

    lanes = WA_GROUP * BLOCK
    return pl.pallas_call(
        functools.partial(_wa_body, npair=npair, steps=steps),
        grid=(steps + 1,),
        in_specs=[pair_spec(ATT_WIDTH, scored, rq),
                  k_spec(1, before), k_spec(WA_TILES, lambda pair: pair), k_spec(1, after),
                  v_spec(before), pair_spec(WA_KV_WIDTH, finished, rv), v_spec(after),
                  pl.BlockSpec((None, N_META, WA_KV_WIDTH), lambda s: (0, 0, 0)),
                  pl.BlockSpec((None, None, WA_KV_WIDTH, N_META), lambda s: (0, 0, rv, 0)),
                  pl.BlockSpec(band.shape, lambda s: (0, 0, 0, 0)),
                  pl.BlockSpec((WA_SUB, WA_KV_HEADS, N_META, lanes),
                               lambda s: (scored(s)[1], 0, 0, 0)),
                  pl.BlockSpec((WA_KV_HEADS, 1, lanes), lambda s: (0, 0, 0))],
        out_specs=pair_spec(ATT_WIDTH, finished, 0),
        out_shape=jax.ShapeDtypeStruct((b, nt, ATT_WIDTH, TOKEN_TILE), BF16),
        scratch_shapes=[pltpu.VMEM((WA_SUB, WA_KV_HEADS, WA_KEYS, lanes), F32),
                        pltpu.VMEM((WA_SUB, WA_KV_HEADS, 8, lanes), F32)] * 2,
        compiler_params=pltpu.CompilerParams(
            dimension_semantics=("arbitrary",), vmem_limit_bytes=VMEM_LIMIT),
        name="wa_attend",
    )(pt, kb, kb, kb, pt, pt, pt, meta_kb, meta_pt, band, meta_bias, sink_row)


def _twice_sigmoid_of_twice(half_x):
    return jnp.tanh(half_x) + 1.0


def _tiles_side_by_side(ref):
    return jnp.concatenate([ref[t] for t in range(ref.shape[0])], axis=1)


def _mix_out_body(x_ref, oa_ref, ob_ref, za_ref, zb_ref, ga_ref, gb_ref,
                  wpa_ref, wpb_ref, wout_ref, fg_ref, y_ref, merged0_ref, merged1_ref):
    step = pl.program_id(0)

    @pl.when(step == 0)
    def _():
        merged1_ref[...] = jnp.zeros_like(merged1_ref)

    def stages(merged_prev, merged_next):
        h = x_ref[...] + lax.dot_general(merged_prev[...], wout_ref[...], _TN,
                                         preferred_element_type=F32)
        ms = jnp.mean(h * h, axis=-1, keepdims=True)
        y_ref[...] = (h * lax.rsqrt(ms + RMS_EPS)) * fg_ref[...]
        za = _tiles_side_by_side(za_ref).astype(F32)
        zb = _tiles_side_by_side(zb_ref).astype(F32)
        ta = (_tiles_side_by_side(oa_ref).astype(F32)
              * (za * _twice_sigmoid_of_twice(za))).astype(BF16)
        tb = (_tiles_side_by_side(ob_ref).astype(F32)
              * (zb * _twice_sigmoid_of_twice(zb))).astype(BF16)
        ya = jnp.dot(wpa_ref[...], ta, preferred_element_type=F32)
        yb = jnp.dot(wpb_ref[...], tb, preferred_element_type=F32)
        merged_next[...] = (
            _twice_sigmoid_of_twice(_tiles_side_by_side(ga_ref).astype(F32)) * ya
            + _twice_sigmoid_of_twice(_tiles_side_by_side(gb_ref).astype(F32)) * yb).astype(BF16)

    @pl.when(step % 2 == 0)
    def _():
        stages(merged1_ref, merged0_ref)

    @pl.when(step % 2 == 1)
    def _():
        stages(merged0_ref, merged1_ref)


def _mix_out(x, oa, ob, pt, wpa_t, wpb_t, wout_half, fg, tm):
    b, n, _ = x.shape
    tile = pt.shape[-1]
    assert n % tm == 0 and tm % tile == 0 and oa.shape == ob.shape == (b, n // tile, ATT_WIDTH, tile)
    steps = b * n // tm
    per_step = tm // tile
    x_tiles = x.reshape(steps, tm, D_MODEL)

    def tiles(a):
        return a.reshape((steps * per_step,) + a.shape[2:])

    def feat(rows, row0):
        return pl.BlockSpec((per_step, rows, tile),
                            lambda s: (jnp.minimum(s, steps - 1), row0 // rows, 0))

    def whole(shape):
        return pl.BlockSpec(shape, lambda s: (0, 0))

    tok = pl.BlockSpec((None, tm, D_MODEL), lambda s: (jnp.maximum(s - 1, 0), 0, 0))
    y = pl.pallas_call(
        _mix_out_body,
        grid=(steps + 1,),
        in_specs=[tok, feat(ATT_WIDTH, 0), feat(ATT_WIDTH, 0),
                  feat(ATT_WIDTH, ROW_ZA), feat(ATT_WIDTH, ROW_ZB),
                  feat(D_MODEL, ROW_GA), feat(D_MODEL, ROW_GB),
                  whole((D_MODEL, ATT_WIDTH)), whole((D_MODEL, ATT_WIDTH)),
                  whole((D_MODEL, D_MODEL)), whole((1, D_MODEL))],
        out_specs=tok,
        out_shape=jax.ShapeDtypeStruct((steps, tm, D_MODEL), F32),
        scratch_shapes=[pltpu.VMEM((D_MODEL, tm), BF16)] * 2,
        compiler_params=pltpu.CompilerParams(
            dimension_semantics=("arbitrary",), vmem_limit_bytes=VMEM_LIMIT),
        name="mix_out",
    )(x_tiles, tiles(oa), tiles(ob), tiles(pt), tiles(pt), tiles(pt), tiles(pt),
      wpa_t, wpb_t, wout_half, fg)
    return y.reshape(b, n, D_MODEL)


def _encode(x, meta, params):
    n = x.shape[1]
    meta_ka, meta_kb, meta_pt = meta
    ka, kb, pt = _norm_proj(x, params["norm_g"], params["w_k"], params["w_t"], PROJ_TM, TOKEN_TILE)
    oa = _na_attend(ka, pt, meta_ka, meta_pt, params["na_bias"])
    band, meta_bias, sink_row = params["wa_bias"]
    assert meta_bias.shape[0] >= n // BLOCK
    ob = _wa_attend(kb, pt, meta_kb, meta_pt, band, meta_bias, sink_row)
    return _mix_out(x, oa, ob, pt, params["w_proj_a_t"], params["w_proj_b_t"], params["w_out_half"],
                    params["final_g"], MIX_TM)


def kernel(x_prompt, x_sample, meta_tokens, norm_g, w_in, na_rpb, sink_logit, w_proj_a, w_proj_b,
           w_out, t5_bias, final_g):
    assert norm_g.shape[0] == 1, "one layer"
    w = w_in[0]
    scale = HEAD_DIM ** -0.5 * LOG2_E
    col = {"qA": (0, 512, scale), "kA": (512, 1024, 1.0), "vA": (1024, 1536, 1.0),
           "zA": (1536, 2048, 0.5), "qB": (2048, 2560, scale), "kB": (2560, 2688, 1.0),
           "vB": (2688, 2816, 1.0), "zB": (2816, 3328, 0.5), "gA": (3328, 4352, 0.5),
           "gB": (4352, 5376, 0.5)}
    w_rows = jnp.transpose(w)
    feature_major = ("gA", "gB", "qA", "vA", "zA", "qB", "zB", "vB")
    w_t = jnp.concatenate([w_rows[col[c][0]:col[c][1]] * col[c][2] for c in feature_major], axis=0)
    max_blocks = max(x_prompt.shape[1], x_sample.shape[1]) // BLOCK
    params = {
        "norm_g": norm_g[0].reshape(1, D_MODEL).astype(F32),
        "w_k": jnp.concatenate([w[:, 512:1024], w[:, 2560:2688]], axis=1).astype(BF16),
        "w_t": w_t.astype(BF16),
        "na_bias": _na_bias_tables(na_rpb[0]),
        "wa_bias": _wa_bias_tables(t5_bias, sink_logit[0], max_blocks),
        "w_proj_a_t": jnp.transpose(w_proj_a[0]).astype(BF16),
        "w_proj_b_t": jnp.transpose(w_proj_b[0]).astype(BF16),
        "w_out_half": (w_out[0] * 0.5).astype(BF16),
        "final_g": final_g.reshape(1, D_MODEL).astype(F32),
    }
    meta = _norm_proj(meta_tokens.astype(F32)[None], params["norm_g"],
                      params["w_k"], params["w_t"], N_META, N_META)
    return (_encode(x_prompt, meta, params), _encode(x_sample, meta, params))
```

```python
import functools
import math

import jax
import jax.numpy as jnp
from jax import lax
from jax.experimental import pallas as pl
from jax.experimental.pallas import tpu as pltpu

F32 = jnp.float32
BF16 = jnp.bfloat16

D_MODEL = 1024
N_META = 16
GRID_W = 64
HEADS = 8
HEAD_DIM = 64
ATT_WIDTH = HEADS * HEAD_DIM
NA_WIN_ROWS = 8
NA_WIN_COLS = 16
WA_KV_HEADS = 2
WA_GROUP = HEADS // WA_KV_HEADS
WA_KV_WIDTH = WA_KV_HEADS * HEAD_DIM
WINDOW = 128
BLOCK = 128
T5_BUCKETS = 32
T5_MAX_DIST = 128
RMS_EPS = 1e-6
NEG_INF = -1e30
LOG2_E = math.log2(math.e)

K_WIDTH = ATT_WIDTH + WA_KV_WIDTH
ROW_GA, ROW_GB, ROW_QA, ROW_VA, ROW_ZA, ROW_QB, ROW_ZB, ROW_VB = (
    0, 1024, 2048, 2560, 3072, 3584, 4096, 4608)
T_WIDTH = 4736

NA_ROWS_PER_STEP = 4
NA_Q = NA_ROWS_PER_STEP * GRID_W
NA_KEY_SEGS = 3
PAIR = 2 * HEAD_DIM
TOKEN_TILE = 256
PROJ_TM = 512
PROJ_ROW_CHUNK = 1184
MIX_TM = 512
VMEM_LIMIT = 52 * 1024 * 1024

_NT = (((1,), (1,)), ((), ()))
_TN = (((0,), (0,)), ((), ()))


def _t5_bucket(rel):
    half = T5_BUCKETS // 2
    exact = half // 2
    ret = jnp.where(rel > 0, half, 0)
    n = jnp.abs(rel)
    nf = jnp.maximum(n, 1).astype(F32)
    large = exact + (jnp.log(nf / exact) / math.log(T5_MAX_DIST / exact)
                     * (half - exact)).astype(jnp.int32)
    large = jnp.minimum(large, half - 1)
    return ret + jnp.where(n < exact, n, large)


def _t5_lookup(tab, rel):
    bucket = _t5_bucket(rel)
    tail = (1,) * bucket.ndim
    hit = bucket[None] == jnp.arange(T5_BUCKETS).reshape((T5_BUCKETS,) + tail)
    vals = jnp.transpose(tab).reshape((tab.shape[1], T5_BUCKETS) + tail)
    return jnp.sum(jnp.where(hit[None], vals, 0.0), axis=1)


def _toeplitz(v, rows, cols):
    length = rows + cols - 1
    assert v.shape[-1] == length
    lead = v.shape[:-1]
    flat = jnp.broadcast_to(v[..., None, :], lead + (rows, length)).reshape(lead + (rows * length,))
    flat = flat[..., rows - 1:rows - 1 + rows * (length - 1)]
    return flat.reshape(lead + (rows, length - 1))[..., :cols]


ONES_ROWS = 16


def _with_ones_rows(v):
    return jnp.concatenate([v, jnp.ones((ONES_ROWS, v.shape[1]), v.dtype)], axis=0)


def _half_padded(q, upper):
    zeros = jnp.zeros_like(q)
    return jnp.concatenate([zeros, q] if upper else [q, zeros], axis=0)


def _norm_proj_body(x_ref, g_ref, wk_ref, wt_ref, oka_ref, okb_ref, ot_ref):
    x = x_ref[...]
    ms = jnp.mean(x * x, axis=-1, keepdims=True)
    u = ((x * lax.rsqrt(ms + RMS_EPS)) * g_ref[...]).astype(BF16)
    keys = jnp.dot(u, wk_ref[...], preferred_element_type=F32).astype(BF16)
    oka_ref[...] = keys[:, :ATT_WIDTH]
    okb_ref[...] = keys[:, ATT_WIDTH:]
    n_tiles, _, tile = ot_ref.shape[1:]
    for r in range(0, T_WIDTH, PROJ_ROW_CHUNK):
        rows = lax.dot_general(wt_ref[r:r + PROJ_ROW_CHUNK, :], u, _NT,
                               preferred_element_type=F32).astype(BF16)
        for t in range(n_tiles):
            ot_ref[0, t, r:r + PROJ_ROW_CHUNK, :] = rows[:, t * tile:(t + 1) * tile]


def _norm_proj(x, g, wk, wt, tm, tile):
    b, n, _ = x.shape
    assert n % tm == 0 and tm % tile == 0
    return pl.pallas_call(
        _norm_proj_body,
        grid=(b, n // tm),
        in_specs=[pl.BlockSpec((None, tm, D_MODEL), lambda bb, i: (bb, i, 0)),
                  pl.BlockSpec((1, D_MODEL), lambda bb, i: (0, 0)),
                  pl.BlockSpec((D_MODEL, K_WIDTH), lambda bb, i: (0, 0)),
                  pl.BlockSpec((T_WIDTH, D_MODEL), lambda bb, i: (0, 0))],
        out_specs=[pl.BlockSpec((None, tm, ATT_WIDTH), lambda bb, i: (bb, i, 0)),
                   pl.BlockSpec((None, tm, WA_KV_WIDTH), lambda bb, i: (bb, i, 0)),
                   pl.BlockSpec((1, tm // tile, T_WIDTH, tile), lambda bb, i: (bb, i, 0, 0))],
        out_shape=[jax.ShapeDtypeStruct((b, n, ATT_WIDTH), BF16),
                   jax.ShapeDtypeStruct((b, n, WA_KV_WIDTH), BF16),
                   jax.ShapeDtypeStruct((b, n // tile, T_WIDTH, tile), BF16)],
        compiler_params=pltpu.CompilerParams(
            dimension_semantics=("arbitrary", "arbitrary"), vmem_limit_bytes=VMEM_LIMIT),
        name="norm_proj",
    )(x, g, wk, wt)


def _na_bias_tables(rpb):
    n_t = NA_KEY_SEGS * NA_ROWS_PER_STEP
    lead = GRID_W - NA_WIN_COLS
    v = jnp.pad(rpb.astype(F32) * LOG2_E, ((0, 0), (0, 0), (lead, lead)))
    cols = _toeplitz(v, GRID_W, GRID_W)
    j = jnp.arange(GRID_W)[:, None]
    c = jnp.arange(GRID_W)[None, :]
    cs = jnp.clip(j - NA_WIN_COLS // 2, 0, GRID_W - NA_WIN_COLS)
    cols = jnp.where((c >= cs) & (c < cs + NA_WIN_COLS), cols, NEG_INF)
    cols_t = jnp.swapaxes(cols, -1, -2)
    top_off = NA_WIN_ROWS // 2 - 1
    assert top_off - (NA_ROWS_PER_STEP - 1) >= 0 and top_off + n_t <= cols.shape[1]
    full = jnp.stack([cols_t[:, top_off - a:top_off - a + n_t]
                      for a in range(NA_ROWS_PER_STEP)], axis=3)
    a = jnp.arange(NA_ROWS_PER_STEP)[None, :]
    t = jnp.arange(NA_WIN_T)[:, None]

    def masked(tab, ok):
        return jnp.where(ok[None, :, None, :, None], tab, NEG_INF)

    first8 = (t < NA_WIN_ROWS) & (a >= 0)
    top = jnp.concatenate([full[:, 2 * NA_ROWS_PER_STEP:], full[:, NA_ROWS_PER_STEP:NA_WIN_T]], axis=1)
    tabs = [masked(top, first8),
            masked(full[:, :NA_WIN_T], (t - a >= 0) & (t - a < NA_WIN_ROWS)),
            masked(full[:, :NA_WIN_T], first8)]
    return jnp.stack(tabs).reshape(3, HEADS, NA_WIN_KEYS, NA_Q)


NA_WIN_T = NA_KEY_SEGS * NA_ROWS_PER_STEP - 1
NA_WIN_KEYS = NA_WIN_T * GRID_W
NA_SEG_KEYS = (NA_Q, NA_Q, NA_WIN_KEYS - 2 * NA_Q)
NA_KEYS = NA_WIN_KEYS + N_META


NA_TILES = 2
NA_META_SLOT = NA_SEG_KEYS[-1] - (NA_Q - BLOCK)


def _na_body(q_ref, kb_ref, kc_ref, ka_ref, vb_ref, vc_ref, va_ref, km_ref, vm_ref,
             bias_ref, o_ref, s_ref, m_ref, *, npair, steps):
    n_win = NA_WIN_KEYS
    step = pl.program_id(0)
    pair = lax.rem(jnp.minimum(step, steps - 1), npair)
    kinds = (jnp.where(pair == 0, 0, 1), jnp.where(pair == npair - 1, 2, 1))

    def key_seg(t, g, used, slab):
        if t + g == 0:
            return kb_ref[:used, slab]
        if t + g == NA_TILES + 1:
            return ka_ref[:used, slab]
        return kc_ref[(t + g - 1) * NA_Q:(t + g - 1) * NA_Q + used, slab]

    def val_seg(t, g, rows):
        if t + g == 0:
            return vb_ref[rows, :]
        if t + g == NA_TILES + 1:
            return va_ref[rows, :]
        return vc_ref[t + g - 1, rows, :]

    @pl.when(step == 0)
    def _():
        s_ref[...] = jnp.zeros_like(s_ref)
        m_ref[...] = jnp.zeros_like(m_ref)

    meta_slots = lax.broadcasted_iota(jnp.int32, (HEAD_DIM, BLOCK), 1) >= NA_META_SLOT

    for t in range(NA_TILES):
        for h in range(HEADS):
            rows = slice(h * HEAD_DIM, (h + 1) * HEAD_DIM)
            m_old = m_ref[t, h, 0:1, :]
            slab = slice((h // 2) * PAIR, (h // 2 + 1) * PAIR)
            q = _half_padded(q_ref[t, rows, :], h % 2)
            s_meta = jnp.dot(km_ref[:, slab], q, preferred_element_type=F32)
            m = jnp.max(s_meta, axis=0, keepdims=True)
            o = None
            for g in range(NA_KEY_SEGS):
                used = NA_SEG_KEYS[g]
                seg = slice(g * NA_Q, g * NA_Q + used)
                v = val_seg(t, g, rows)
                if g < NA_KEY_SEGS - 1:
                    p = jnp.exp2(s_ref[t, h, seg, :] - m_old).astype(BF16)
                else:
                    p = jnp.exp2(s_ref[t, h, g * NA_Q:NA_KEYS, :] - m_old).astype(BF16)
                    p = jnp.concatenate(
                        [p, jnp.zeros(((g + 1) * NA_Q - NA_KEYS, NA_Q), BF16)], axis=0)
                    tail = jnp.where(meta_slots, vm_ref[rows, :], v[:, NA_Q - BLOCK:])
                    v = jnp.concatenate([v[:, :NA_Q - BLOCK], tail], axis=1)
                part = jnp.dot(_with_ones_rows(v), p, preferred_element_type=F32)
                o = part if o is None else o + part
                sg = (jnp.dot(key_seg(t, g, used, slab), q, preferred_element_type=F32)
                      + bias_ref[kinds[t], h, seg, :])
                s_ref[t, h, seg, :] = sg
                m = jnp.maximum(m, jnp.max(sg, axis=0, keepdims=True))
            s_ref[t, h, n_win:NA_KEYS, :] = s_meta
            o_ref[t, rows, :] = (o[:HEAD_DIM] / o[HEAD_DIM:HEAD_DIM + 1]).astype(BF16)
            m_ref[t, h, 0:1, :] = m


def _na_attend(ka, pt, meta_ka, meta_pt, bias_tabs):
    b, n, _ = ka.shape
    nblk = n // NA_Q
    assert pt.shape[1:] == (nblk, T_WIDTH, NA_Q) and nblk % NA_TILES == 0
    assert nblk >= 4
    npair = nblk // NA_TILES
    rq, rv = ROW_QA // ATT_WIDTH, ROW_VA // ATT_WIDTH
    meta_v = jnp.pad(meta_pt[0, 0, ROW_VA:ROW_VA + ATT_WIDTH, :],
                     ((0, 0), (NA_META_SLOT, BLOCK - NA_META_SLOT - N_META)))

    steps = b * npair

    def scored(s):
        return jnp.divmod(jnp.minimum(s, steps - 1), npair)

    def finished(s):
        return jnp.divmod(jnp.maximum(s - 1, 0), npair)

    def before(pair):
        return jnp.where(pair == 0, 1, pair * NA_TILES - 1)

    def after(pair):
        return jnp.minimum((pair + 1) * NA_TILES, nblk - 1)

    def k_spec(tiles, tile_of):
        def index(s):
            bb, pair = scored(s)
            return bb, tile_of(pair), 0
        return pl.BlockSpec((None, tiles * NA_Q, ATT_WIDTH), index)

    def v_spec(tile_of):
        def index(s):
            bb, pair = finished(s)
            return bb, tile_of(pair), rv, 0
        return pl.BlockSpec((None, None, ATT_WIDTH, NA_Q), index)

    def pair_spec(which, row_block):
        def index(s):
            bb, pair = which(s)
            return bb, pair, row_block, 0
        return pl.BlockSpec((None, NA_TILES, ATT_WIDTH, NA_Q), index)

    return pl.pallas_call(
        functools.partial(_na_body, npair=npair, steps=steps),
        grid=(steps + 1,),
        in_specs=[pair_spec(scored, rq),
                  k_spec(1, before), k_spec(NA_TILES, lambda pair: pair), k_spec(1, after),
                  v_spec(before), pair_spec(finished, rv), v_spec(after),
                  pl.BlockSpec((None, N_META, ATT_WIDTH), lambda s: (0, 0, 0)),
                  pl.BlockSpec((ATT_WIDTH, BLOCK), lambda s: (0, 0)),
                  pl.BlockSpec(bias_tabs.shape, lambda s: (0, 0, 0, 0))],
        out_specs=pair_spec(finished, 0),
        out_shape=jax.ShapeDtypeStruct((b, nblk, ATT_WIDTH, NA_Q), BF16),
        scratch_shapes=[pltpu.VMEM((NA_TILES, HEADS, NA_KEYS, NA_Q), F32),
                        pltpu.VMEM((NA_TILES, HEADS, 8, NA_Q), F32)],
        compiler_params=pltpu.CompilerParams(
            dimension_semantics=("arbitrary",), vmem_limit_bytes=VMEM_LIMIT),
        name="na_attend",
    )(pt, ka, ka, ka, pt, pt, pt, meta_ka, meta_v, bias_tabs)


def _wa_bias_tables(t5_bias, sink, nb):
    tab = t5_bias.astype(F32) * LOG2_E
    qq = jnp.arange(BLOCK)
    kk = jnp.arange(3 * BLOCK)
    rel = kk[None, :] - BLOCK - qq[:, None]
    dist = jnp.arange(-(2 * BLOCK - 1), 2 * BLOCK)
    band = _toeplitz(_t5_lookup(tab, dist), BLOCK, 3 * BLOCK)
    in_window = jnp.abs(rel) <= WINDOW
    seg = (kk // BLOCK)[None, :]
    kinds = (in_window & (seg != 0), in_window, in_window & (seg != 2))
    band = jnp.stack([jnp.where(ok[None], band, NEG_INF) for ok in kinds])
    band = band.reshape(3, WA_KV_HEADS, WA_GROUP, BLOCK, 3 * BLOCK)
    band = jnp.transpose(band, (0, 1, 4, 2, 3)).reshape(3, WA_KV_HEADS, 3 * BLOCK, WA_GROUP * BLOCK)
    n = nb * BLOCK
    per_dist = _t5_lookup(tab, -jnp.arange(n + BLOCK + N_META))
    main = per_dist[:, 1:1 + n].reshape(HEADS, nb, BLOCK)
    tail = per_dist[:, 1 + BLOCK:1 + BLOCK + n].reshape(HEADS, nb, BLOCK)[..., :N_META - 1]
    meta = _toeplitz(jnp.concatenate([main, tail], axis=-1), N_META, BLOCK)
    meta = meta.reshape(WA_KV_HEADS, WA_GROUP, nb, N_META, BLOCK)
    meta = jnp.transpose(meta, (2, 0, 3, 1, 4)).reshape(nb, WA_KV_HEADS, N_META, WA_GROUP * BLOCK)
    sink_row = jnp.broadcast_to((sink.astype(F32) * LOG2_E).reshape(WA_KV_HEADS, WA_GROUP, 1),
                                (WA_KV_HEADS, WA_GROUP, BLOCK)).reshape(WA_KV_HEADS, 1, WA_GROUP * BLOCK)
    return band, meta, sink_row


WA_KEYS = 3 * BLOCK + N_META


WA_TILES = 2
WA_SUB = WA_TILES * TOKEN_TILE // BLOCK


def _wa_body(q_ref, kp_ref, kc_ref, kn_ref, vp_ref, vc_ref, vn_ref, km_ref, vm_ref,
             band_ref, mbias_ref, sink_ref, o_ref, s_ref, m_ref, *, npair, steps):
    n_band = 3 * BLOCK
    k_pad = 4 * BLOCK - WA_KEYS
    lo, hi = slice(0, BLOCK), slice(BLOCK, TOKEN_TILE)
    step = pl.program_id(0)
    pair = lax.rem(jnp.minimum(step, steps - 1), npair)
    kinds = ([jnp.where(pair == 0, 0, 1)] + [1] * (WA_SUB - 2)
             + [jnp.where(pair == npair - 1, 2, 1)])

    def key_block(j):
        if j == 0:
            return kp_ref[hi, :]
        if j == WA_SUB + 1:
            return kn_ref[lo, :]
        return kc_ref[(j - 1) * BLOCK:j * BLOCK, :]

    def val_block(j, vrows):
        if j == 0:
            return vp_ref[vrows, hi]
        if j == WA_SUB + 1:
            return vn_ref[vrows, lo]
        return vc_ref[(j - 1) // 2, vrows, (lo, hi)[(j - 1) % 2]]

    @pl.when(step == 0)
    def _():
        s_ref[...] = jnp.zeros_like(s_ref)
        m_ref[...] = jnp.zeros_like(m_ref)

    def stages(s_prev, m_prev, s_next, m_next):
        for sub in range(WA_SUB):
            tile, cols = sub // 2, (lo, hi)[sub % 2]
            for kv in range(WA_KV_HEADS):
                vrows = slice(kv * HEAD_DIM, (kv + 1) * HEAD_DIM)
                m = m_prev[sub, kv, 0:1, :]
                p = jnp.exp2(s_prev[sub, kv, :, :] - m).astype(BF16)
                p = jnp.concatenate([p, jnp.zeros((k_pad, p.shape[1]), BF16)], axis=0)
                v = jnp.concatenate([val_block(sub + g, vrows) for g in range(3)]
                                    + [vm_ref[vrows, :]], axis=1)
                o = jnp.dot(_with_ones_rows(v), p, preferred_element_type=F32)
                l = o[HEAD_DIM:HEAD_DIM + 1] + jnp.exp2(sink_ref[kv] - m)
                o = (o[:HEAD_DIM] / l).astype(BF16)
                for g in range(WA_GROUP):
                    h = kv * WA_GROUP + g
                    o_ref[tile, h * HEAD_DIM:(h + 1) * HEAD_DIM, cols] = (
                        o[:, g * BLOCK:(g + 1) * BLOCK])
                q4 = jnp.concatenate(
                    [q_ref[tile, (kv * WA_GROUP + g) * HEAD_DIM:(kv * WA_GROUP + g + 1) * HEAD_DIM,
                           cols] for g in range(WA_GROUP)], axis=1)
                q4 = _half_padded(q4, kv)
                s_meta = (jnp.dot(km_ref[...], q4, preferred_element_type=F32)
                          + mbias_ref[sub, kv])
                m = jnp.maximum(jnp.max(s_meta, axis=0, keepdims=True), sink_ref[kv])
                for g in range(3):
                    sg = (jnp.dot(key_block(sub + g), q4, preferred_element_type=F32)
                          + band_ref[kinds[sub], kv, g * BLOCK:(g + 1) * BLOCK, :])
                    s_next[sub, kv, g * BLOCK:(g + 1) * BLOCK, :] = sg
                    m = jnp.maximum(m, jnp.max(sg, axis=0, keepdims=True))
                s_next[sub, kv, n_band:WA_KEYS, :] = s_meta
                m_next[sub, kv, 0:1, :] = m

    stages(s_ref, m_ref, s_ref, m_ref)


def _wa_attend(kb, pt, meta_kb, meta_pt, band, meta_bias, sink_row):
    b, n, _ = kb.shape
    nt = n // TOKEN_TILE
    assert pt.shape[1:] == (nt, T_WIDTH, TOKEN_TILE) and nt % WA_TILES == 0
    npair = nt // WA_TILES
    rq, rv = ROW_QB // ATT_WIDTH, ROW_VB // WA_KV_WIDTH
    meta_v = jnp.pad(meta_pt[0, 0, ROW_VB:ROW_VB + WA_KV_WIDTH, :], ((0, 0), (0, BLOCK - N_META)))

    steps = b * npair

    def scored(s):
        return jnp.divmod(jnp.minimum(s, steps - 1), npair)

    def finished(s):
        return jnp.divmod(jnp.maximum(s - 1, 0), npair)

    def before(pair):
        return jnp.maximum(pair * WA_TILES - 1, 0)

    def after(pair):
        return jnp.minimum((pair + 1) * WA_TILES, nt - 1)

    def k_spec(tiles, tile_of):
        def index(s):
            bb, pair = scored(s)
            return bb, tile_of(pair), 0
        return pl.BlockSpec((None, tiles * TOKEN_TILE, WA_KV_WIDTH), index)

    def v_spec(tile_of):
        def index(s):
            bb, pair = finished(s)
            return bb, tile_of(pair), rv, 0
        return pl.BlockSpec((None, None, WA_KV_WIDTH, TOKEN_TILE), index)

    def pair_spec(rows, which, row_block):
        def index(s):
            bb, pair = which(s)
            return bb, pair, row_block, 0
        return pl.BlockSpec((None, WA_TILES, rows, TOKEN_TILE), index)

    lanes = WA_GROUP * BLOCK
    return pl.pallas_call(
        functools.partial(_wa_body, npair=npair, steps=steps),
        grid=(steps + 1,),
        in_specs=[pair_spec(ATT_WIDTH, scored, rq),
                  k_spec(1, before), k_spec(WA_TILES, lambda pair: pair), k_spec(1, after),
                  v_spec(before), pair_spec(WA_KV_WIDTH, finished, rv), v_spec(after),
                  pl.BlockSpec((None, N_META, WA_KV_WIDTH), lambda s: (0, 0, 0)),
                  pl.BlockSpec((WA_KV_WIDTH, BLOCK), lambda s: (0, 0)),
                  pl.BlockSpec(band.shape, lambda s: (0, 0, 0, 0)),
                  pl.BlockSpec((WA_SUB, WA_KV_HEADS, N_META, lanes),
                               lambda s: (scored(s)[1], 0, 0, 0)),
                  pl.BlockSpec((WA_KV_HEADS, 1, lanes), lambda s: (0, 0, 0))],
        out_specs=pair_spec(ATT_WIDTH, finished, 0),
        out_shape=jax.ShapeDtypeStruct((b, nt, ATT_WIDTH, TOKEN_TILE), BF16),
        scratch_shapes=[pltpu.VMEM((WA_SUB, WA_KV_HEADS, WA_KEYS, lanes), F32),
                        pltpu.VMEM((WA_SUB, WA_KV_HEADS, 8, lanes), F32)],
        compiler_params=pltpu.CompilerParams(
            dimension_semantics=("arbitrary",), vmem_limit_bytes=VMEM_LIMIT),
        name="wa_attend",
    )(pt, kb, kb, kb, pt, pt, pt, meta_kb, meta_v, band, meta_bias, sink_row)


def _twice_sigmoid_of_twice(half_x):
    return jnp.tanh(half_x) + 1.0


def _tiles_side_by_side(ref):
    return jnp.concatenate([ref[t] for t in range(ref.shape[0])], axis=1)


def _mix_out_body(x_ref, oa_ref, ob_ref, za_ref, zb_ref, ga_ref, gb_ref,
                  wpa_ref, wpb_ref, wout_ref, fg_ref, y_ref, merged0_ref, merged1_ref):
    step = pl.program_id(0)

    @pl.when(step == 0)
    def _():
        merged1_ref[...] = jnp.zeros_like(merged1_ref)

    def stages(merged_prev, merged_next):
        h = x_ref[...] + lax.dot_general(merged_prev[...], wout_ref[...], _TN,
                                         preferred_element_type=F32)
        ms = jnp.mean(h * h, axis=-1, keepdims=True)
        y_ref[...] = (h * lax.rsqrt(ms + RMS_EPS)) * fg_ref[...]
        za = _tiles_side_by_side(za_ref).astype(F32)
        zb = _tiles_side_by_side(zb_ref).astype(F32)
        ta = (_tiles_side_by_side(oa_ref).astype(F32)
              * (za * _twice_sigmoid_of_twice(za))).astype(BF16)
        tb = (_tiles_side_by_side(ob_ref).astype(F32)
              * (zb * _twice_sigmoid_of_twice(zb))).astype(BF16)
        ya = jnp.dot(wpa_ref[...], ta, preferred_element_type=F32)
        yb = jnp.dot(wpb_ref[...], tb, preferred_element_type=F32)
        merged_next[...] = (
            _twice_sigmoid_of_twice(_tiles_side_by_side(ga_ref).astype(F32)) * ya
            + _twice_sigmoid_of_twice(_tiles_side_by_side(gb_ref).astype(F32)) * yb).astype(BF16)

    @pl.when(step % 2 == 0)
    def _():
        stages(merged1_ref, merged0_ref)

    @pl.when(step % 2 == 1)
    def _():
        stages(merged0_ref, merged1_ref)


def _mix_out(x, oa, ob, pt, wpa_t, wpb_t, wout_half, fg, tm):
    b, n, _ = x.shape
    tile = pt.shape[-1]
    assert n % tm == 0 and tm % tile == 0 and oa.shape == ob.shape == (b, n // tile, ATT_WIDTH, tile)
    steps = b * n // tm
    per_step = tm // tile
    x_tiles = x.reshape(steps, tm, D_MODEL)

    def tiles(a):
        return a.reshape((steps * per_step,) + a.shape[2:])

    def feat(rows, row0):
        return pl.BlockSpec((per_step, rows, tile),
                            lambda s: (jnp.minimum(s, steps - 1), row0 // rows, 0))

    def whole(shape):
        return pl.BlockSpec(shape, lambda s: (0, 0))

    tok = pl.BlockSpec((None, tm, D_MODEL), lambda s: (jnp.maximum(s - 1, 0), 0, 0))
    y = pl.pallas_call(
        _mix_out_body,
        grid=(steps + 1,),
        in_specs=[tok, feat(ATT_WIDTH, 0), feat(ATT_WIDTH, 0),
                  feat(ATT_WIDTH, ROW_ZA), feat(ATT_WIDTH, ROW_ZB),
                  feat(D_MODEL, ROW_GA), feat(D_MODEL, ROW_GB),
                  whole((D_MODEL, ATT_WIDTH)), whole((D_MODEL, ATT_WIDTH)),
                  whole((D_MODEL, D_MODEL)), whole((1, D_MODEL))],
        out_specs=tok,
        out_shape=jax.ShapeDtypeStruct((steps, tm, D_MODEL), F32),
        scratch_shapes=[pltpu.VMEM((D_MODEL, tm), BF16)] * 2,
        compiler_params=pltpu.CompilerParams(
            dimension_semantics=("arbitrary",), vmem_limit_bytes=VMEM_LIMIT),
        name="mix_out",
    )(x_tiles, tiles(oa), tiles(ob), tiles(pt), tiles(pt), tiles(pt), tiles(pt),
      wpa_t, wpb_t, wout_half, fg)
    return y.reshape(b, n, D_MODEL)


def _encode(x, meta, params):
    n = x.shape[1]
    meta_ka, meta_kb, meta_pt = meta
    ka, kb, pt = _norm_proj(x, params["norm_g"], params["w_k"], params["w_t"], PROJ_TM, TOKEN_TILE)
    oa = _na_attend(ka, pt, meta_ka, meta_pt, params["na_bias"])
    band, meta_bias, sink_row = params["wa_bias"]
    assert meta_bias.shape[0] >= n // BLOCK
    ob = _wa_attend(kb, pt, meta_kb, meta_pt, band, meta_bias, sink_row)
    return _mix_out(x, oa, ob, pt, params["w_proj_a_t"], params["w_proj_b_t"], params["w_out_half"],
                    params["final_g"], MIX_TM)


def kernel(x_prompt, x_sample, meta_tokens, norm_g, w_in, na_rpb, sink_logit, w_proj_a, w_proj_b,
           w_out, t5_bias, final_g):
    assert norm_g.shape[0] == 1, "one layer"
    w = w_in[0]
    scale = HEAD_DIM ** -0.5 * LOG2_E
    col = {"qA": (0, 512, scale), "kA": (512, 1024, 1.0), "vA": (1024, 1536, 1.0),
           "zA": (1536, 2048, 0.5), "qB": (2048, 2560, scale), "kB": (2560, 2688, 1.0),
           "vB": (2688, 2816, 1.0), "zB": (2816, 3328, 0.5), "gA": (3328, 4352, 0.5),
           "gB": (4352, 5376, 0.5)}
    w_rows = jnp.transpose(w)
    feature_major = ("gA", "gB", "qA", "vA", "zA", "qB", "zB", "vB")
    w_t = jnp.concatenate([w_rows[col[c][0]:col[c][1]] * col[c][2] for c in feature_major], axis=0)
    max_blocks = max(x_prompt.shape[1], x_sample.shape[1]) // BLOCK
    params = {
        "norm_g": norm_g[0].reshape(1, D_MODEL).astype(F32),
        "w_k": jnp.concatenate([w[:, 512:1024], w[:, 2560:2688]], axis=1).astype(BF16),
        "w_t": w_t.astype(BF16),
        "na_bias": _na_bias_tables(na_rpb[0]),
        "wa_bias": _wa_bias_tables(t5_bias, sink_logit[0], max_blocks),
        "w_proj_a_t": jnp.transpose(w_proj_a[0]).astype(BF16),
        "w_proj_b_t": jnp.transpose(w_proj_b[0]).astype(BF16),
        "w_out_half": (w_out[0] * 0.5).astype(BF16),
        "final_g": final_g.reshape(1, D_MODEL).astype(F32),
    }
    meta = _norm_proj(meta_tokens.astype(F32)[None], params["norm_g"],
                      params["w_k"], params["w_t"], N_META, N_META)
    return (_encode(x_prompt, meta, params), _encode(x_sample, meta, params))
```

```python
import functools
import math

import jax
import jax.numpy as jnp
from jax import lax
from jax.experimental import pallas as pl
from jax.experimental.pallas import tpu as pltpu

F32 = jnp.float32
BF16 = jnp.bfloat16

D_MODEL = 1024
N_META = 16
GRID_W = 64
HEADS = 8
HEAD_DIM = 64
ATT_WIDTH = HEADS * HEAD_DIM
NA_WIN_ROWS = 8
NA_WIN_COLS = 16
WA_KV_HEADS = 2
WA_GROUP = HEADS // WA_KV_HEADS
WA_KV_WIDTH = WA_KV_HEADS * HEAD_DIM
WINDOW = 128
BLOCK = 128
T5_BUCKETS = 32
T5_MAX_DIST = 128
RMS_EPS = 1e-6
NEG_INF = -1e30
LOG2_E = math.log2(math.e)

K_WIDTH = ATT_WIDTH + WA_KV_WIDTH
ROW_GA, ROW_GB, ROW_QA, ROW_VA, ROW_ZA, ROW_QB, ROW_ZB, ROW_VB = (
    0, 1024, 2048, 2560, 3072, 3584, 4096, 4608)
T_WIDTH = 4736

NA_ROWS_PER_STEP = 4
NA_Q = NA_ROWS_PER_STEP * GRID_W
NA_KEY_SEGS = 3
PAIR = 2 * HEAD_DIM
TOKEN_TILE = 256
PROJ_TM = 512
PROJ_ROW_CHUNK = 1184
MIX_TM = 512
VMEM_LIMIT = 52 * 1024 * 1024

_NT = (((1,), (1,)), ((), ()))
_TN = (((0,), (0,)), ((), ()))


def _t5_bucket(rel):
    half = T5_BUCKETS // 2
    exact = half // 2
    ret = jnp.where(rel > 0, half, 0)
    n = jnp.abs(rel)
    nf = jnp.maximum(n, 1).astype(F32)
    large = exact + (jnp.log(nf / exact) / math.log(T5_MAX_DIST / exact)
                     * (half - exact)).astype(jnp.int32)
    large = jnp.minimum(large, half - 1)
    return ret + jnp.where(n < exact, n, large)


def _t5_lookup(tab, rel):
    return jnp.moveaxis(tab[_t5_bucket(rel)], -1, 0)


def _toeplitz(v, rows, cols):
    length = rows + cols - 1
    assert v.shape[-1] == length
    lead = v.shape[:-1]
    flat = jnp.broadcast_to(v[..., None, :], lead + (rows, length)).reshape(lead + (rows * length,))
    flat = flat[..., rows - 1:rows - 1 + rows * (length - 1)]
    return flat.reshape(lead + (rows, length - 1))[..., :cols]


ONES_ROWS = 16


def _with_ones_rows(v):
    return jnp.concatenate([v, jnp.ones((ONES_ROWS, v.shape[1]), v.dtype)], axis=0)


def _half_padded(q, upper):
    zeros = jnp.zeros_like(q)
    return jnp.concatenate([zeros, q] if upper else [q, zeros], axis=0)


def _norm_proj_body(x_ref, g_ref, wk_ref, wt_ref, oka_ref, okb_ref, ot_ref):
    x = x_ref[...]
    ms = jnp.mean(x * x, axis=-1, keepdims=True)
    u = ((x * lax.rsqrt(ms + RMS_EPS)) * g_ref[...]).astype(BF16)
    keys = jnp.dot(u, wk_ref[...], preferred_element_type=F32).astype(BF16)
    oka_ref[...] = keys[:, :ATT_WIDTH]
    okb_ref[...] = keys[:, ATT_WIDTH:]
    n_tiles, _, tile = ot_ref.shape[1:]
    for r in range(0, T_WIDTH, PROJ_ROW_CHUNK):
        rows = lax.dot_general(wt_ref[r:r + PROJ_ROW_CHUNK, :], u, _NT,
                               preferred_element_type=F32).astype(BF16)
        for t in range(n_tiles):
            ot_ref[0, t, r:r + PROJ_ROW_CHUNK, :] = rows[:, t * tile:(t + 1) * tile]


def _norm_proj(x, g, wk, wt, tm, tile):
    b, n, _ = x.shape
    assert n % tm == 0 and tm % tile == 0
    return pl.pallas_call(
        _norm_proj_body,
        grid=(b, n // tm),
        in_specs=[pl.BlockSpec((None, tm, D_MODEL), lambda bb, i: (bb, i, 0)),
                  pl.BlockSpec((1, D_MODEL), lambda bb, i: (0, 0)),
                  pl.BlockSpec((D_MODEL, K_WIDTH), lambda bb, i: (0, 0)),
                  pl.BlockSpec((T_WIDTH, D_MODEL), lambda bb, i: (0, 0))],
        out_specs=[pl.BlockSpec((None, tm, ATT_WIDTH), lambda bb, i: (bb, i, 0)),
                   pl.BlockSpec((None, tm, WA_KV_WIDTH), lambda bb, i: (bb, i, 0)),
                   pl.BlockSpec((1, tm // tile, T_WIDTH, tile), lambda bb, i: (bb, i, 0, 0))],
        out_shape=[jax.ShapeDtypeStruct((b, n, ATT_WIDTH), BF16),
                   jax.ShapeDtypeStruct((b, n, WA_KV_WIDTH), BF16),
                   jax.ShapeDtypeStruct((b, n // tile, T_WIDTH, tile), BF16)],
        compiler_params=pltpu.CompilerParams(
            dimension_semantics=("arbitrary", "arbitrary"), vmem_limit_bytes=VMEM_LIMIT),
        name="norm_proj",
    )(x, g, wk, wt)


def _na_bias_tables(rpb):
    n_t = NA_KEY_SEGS * NA_ROWS_PER_STEP
    lead = GRID_W - NA_WIN_COLS
    v = jnp.pad(rpb.astype(F32) * LOG2_E, ((0, 0), (0, 0), (lead, lead)))
    cols = _toeplitz(v, GRID_W, GRID_W)
    j = jnp.arange(GRID_W)[:, None]
    c = jnp.arange(GRID_W)[None, :]
    cs = jnp.clip(j - NA_WIN_COLS // 2, 0, GRID_W - NA_WIN_COLS)
    cols = jnp.where((c >= cs) & (c < cs + NA_WIN_COLS), cols, NEG_INF)
    cols_t = jnp.swapaxes(cols, -1, -2)
    top_off = NA_WIN_ROWS // 2 - 1
    assert top_off - (NA_ROWS_PER_STEP - 1) >= 0 and top_off + n_t <= cols.shape[1]
    full = jnp.concatenate([cols_t[:, top_off - a:top_off - a + n_t]
                            for a in range(NA_ROWS_PER_STEP)], axis=-1)
    a = jnp.arange(NA_ROWS_PER_STEP)[None, :]
    t = jnp.arange(NA_WIN_T)[:, None]

    def masked(tab, ok):
        ok_lanes = jnp.repeat(ok, GRID_W, axis=1)
        return jnp.where(ok_lanes[None, :, None, :], tab, NEG_INF)

    first8 = (t < NA_WIN_ROWS) & (a >= 0)
    top = jnp.concatenate([full[:, 2 * NA_ROWS_PER_STEP:], full[:, NA_ROWS_PER_STEP:NA_WIN_T]], axis=1)
    tabs = [masked(top, first8),
            masked(full[:, :NA_WIN_T], (t - a >= 0) & (t - a < NA_WIN_ROWS)),
            masked(full[:, :NA_WIN_T], first8)]
    return jnp.stack(tabs).reshape(3, HEADS, NA_WIN_KEYS, NA_Q)


NA_WIN_T = NA_KEY_SEGS * NA_ROWS_PER_STEP - 1
NA_WIN_KEYS = NA_WIN_T * GRID_W
NA_SEG_KEYS = (NA_Q, NA_Q, NA_WIN_KEYS - 2 * NA_Q)
NA_KEYS = NA_WIN_KEYS + N_META


NA_TILES = 2
NA_META_SLOT = NA_SEG_KEYS[-1] - (NA_Q - BLOCK)


def _na_body(q_ref, kb_ref, kc_ref, ka_ref, vb_ref, vc_ref, va_ref, km_ref, vm_ref,
             bias_ref, o_ref, s_ref, m_ref, *, npair, steps):
    n_win = NA_WIN_KEYS
    step = pl.program_id(0)
    pair = lax.rem(jnp.minimum(step, steps - 1), npair)
    kinds = (jnp.where(pair == 0, 0, 1), jnp.where(pair == npair - 1, 2, 1))

    def key_seg(t, g, used, slab):
        if t + g == 0:
            return kb_ref[:used, slab]
        if t + g == NA_TILES + 1:
            return ka_ref[:used, slab]
        return kc_ref[(t + g - 1) * NA_Q:(t + g - 1) * NA_Q + used, slab]

    def val_seg(t, g, rows):
        if t + g == 0:
            return vb_ref[rows, :]
        if t + g == NA_TILES + 1:
            return va_ref[rows, :]
        return vc_ref[t + g - 1, rows, :]

    @pl.when(step == 0)
    def _():
        s_ref[...] = jnp.zeros_like(s_ref)
        m_ref[...] = jnp.zeros_like(m_ref)

    meta_slots = lax.broadcasted_iota(jnp.int32, (HEAD_DIM, BLOCK), 1) >= NA_META_SLOT

    for t in range(NA_TILES):
        for h in range(HEADS):
            rows = slice(h * HEAD_DIM, (h + 1) * HEAD_DIM)
            m_old = m_ref[t, h, 0:1, :]
            slab = slice((h // 2) * PAIR, (h // 2 + 1) * PAIR)
            q = _half_padded(q_ref[t, rows, :], h % 2)
            s_meta = jnp.dot(km_ref[:, slab], q, preferred_element_type=F32)
            m = jnp.max(s_meta, axis=0, keepdims=True)
            o = None
            for g in range(NA_KEY_SEGS):
                used = NA_SEG_KEYS[g]
                seg = slice(g * NA_Q, g * NA_Q + used)
                v = val_seg(t, g, rows)
                if g < NA_KEY_SEGS - 1:
                    p = jnp.exp2(s_ref[t, h, seg, :] - m_old).astype(BF16)
                else:
                    p = jnp.exp2(s_ref[t, h, g * NA_Q:NA_KEYS, :] - m_old).astype(BF16)
                    p = jnp.concatenate(
                        [p, jnp.zeros(((g + 1) * NA_Q - NA_KEYS, NA_Q), BF16)], axis=0)
                    tail = jnp.where(meta_slots, vm_ref[rows, :], v[:, NA_Q - BLOCK:])
                    v = jnp.concatenate([v[:, :NA_Q - BLOCK], tail], axis=1)
                part = jnp.dot(_with_ones_rows(v), p, preferred_element_type=F32)
                o = part if o is None else o + part
                sg = (jnp.dot(key_seg(t, g, used, slab), q, preferred_element_type=F32)
                      + bias_ref[kinds[t], h, seg, :])
                s_ref[t, h, seg, :] = sg
                m = jnp.maximum(m, jnp.max(sg, axis=0, keepdims=True))
            s_ref[t, h, n_win:NA_KEYS, :] = s_meta
            o_ref[t, rows, :] = (o[:HEAD_DIM] / o[HEAD_DIM:HEAD_DIM + 1]).astype(BF16)
            m_ref[t, h, 0:1, :] = m


def _na_attend(ka, pt, meta_ka, meta_pt, bias_tabs):
    b, n, _ = ka.shape
    nblk = n // NA_Q
    assert pt.shape[1:] == (nblk, T_WIDTH, NA_Q) and nblk % NA_TILES == 0
    assert nblk >= 4
    npair = nblk // NA_TILES
    rq, rv = ROW_QA // ATT_WIDTH, ROW_VA // ATT_WIDTH
    meta_v = jnp.pad(meta_pt[0, 0, ROW_VA:ROW_VA + ATT_WIDTH, :],
                     ((0, 0), (NA_META_SLOT, BLOCK - NA_META_SLOT - N_META)))

    steps = b * npair

    def scored(s):
        return jnp.divmod(jnp.minimum(s, steps - 1), npair)

    def finished(s):
        return jnp.divmod(jnp.maximum(s - 1, 0), npair)

    def before(pair):
        return jnp.where(pair == 0, 1, pair * NA_TILES - 1)

    def after(pair):
        return jnp.minimum((pair + 1) * NA_TILES, nblk - 1)

    def k_spec(tiles, tile_of):
        def index(s):
            bb, pair = scored(s)
            return bb, tile_of(pair), 0
        return pl.BlockSpec((None, tiles * NA_Q, ATT_WIDTH), index)

    def v_spec(tile_of):
        def index(s):
            bb, pair = finished(s)
            return bb, tile_of(pair), rv, 0
        return pl.BlockSpec((None, None, ATT_WIDTH, NA_Q), index)

    def pair_spec(which, row_block):
        def index(s):
            bb, pair = which(s)
            return bb, pair, row_block, 0
        return pl.BlockSpec((None, NA_TILES, ATT_WIDTH, NA_Q), index)

    return pl.pallas_call(
        functools.partial(_na_body, npair=npair, steps=steps),
        grid=(steps + 1,),
        in_specs=[pair_spec(scored, rq),
                  k_spec(1, before), k_spec(NA_TILES, lambda pair: pair), k_spec(1, after),
                  v_spec(before), pair_spec(finished, rv), v_spec(after),
                  pl.BlockSpec((None, N_META, ATT_WIDTH), lambda s: (0, 0, 0)),
                  pl.BlockSpec((ATT_WIDTH, BLOCK), lambda s: (0, 0)),
                  pl.BlockSpec(bias_tabs.shape, lambda s: (0, 0, 0, 0))],
        out_specs=pair_spec(finished, 0),
        out_shape=jax.ShapeDtypeStruct((b, nblk, ATT_WIDTH, NA_Q), BF16),
        scratch_shapes=[pltpu.VMEM((NA_TILES, HEADS, NA_KEYS, NA_Q), F32),
                        pltpu.VMEM((NA_TILES, HEADS, 8, NA_Q), F32)],
        compiler_params=pltpu.CompilerParams(
            dimension_semantics=("arbitrary",), vmem_limit_bytes=VMEM_LIMIT),
        name="na_attend",
    )(pt, ka, ka, ka, pt, pt, pt, meta_ka, meta_v, bias_tabs)


def _wa_bias_tables(t5_bias, sink, nb):
    tab = t5_bias.astype(F32) * LOG2_E
    qq = jnp.arange(BLOCK)
    kk = jnp.arange(3 * BLOCK)
    rel = kk[None, :] - BLOCK - qq[:, None]
    dist = jnp.arange(-(2 * BLOCK - 1), 2 * BLOCK)
    band = _toeplitz(_t5_lookup(tab, dist), BLOCK, 3 * BLOCK)
    in_window = jnp.abs(rel) <= WINDOW
    seg = (kk // BLOCK)[None, :]
    kinds = (in_window & (seg != 0), in_window, in_window & (seg != 2))
    band = jnp.stack([jnp.where(ok[None], band, NEG_INF) for ok in kinds])
    band = band.reshape(3, WA_KV_HEADS, WA_GROUP, BLOCK, 3 * BLOCK)
    band = jnp.transpose(band, (0, 1, 4, 2, 3)).reshape(3, WA_KV_HEADS, 3 * BLOCK, WA_GROUP * BLOCK)
    n = nb * BLOCK
    per_dist = _t5_lookup(tab, -jnp.arange(n + N_META + 1))
    meta = jnp.stack([per_dist[:, N_META - m:N_META - m + n].reshape(HEADS, nb, BLOCK)
                      for m in range(N_META)], axis=2)
    meta = meta.reshape(WA_KV_HEADS, WA_GROUP, nb, N_META, BLOCK)
    meta = jnp.transpose(meta, (2, 0, 3, 1, 4)).reshape(nb, WA_KV_HEADS, N_META, WA_GROUP * BLOCK)
    sink_row = jnp.broadcast_to((sink.astype(F32) * LOG2_E).reshape(WA_KV_HEADS, WA_GROUP, 1),
                                (WA_KV_HEADS, WA_GROUP, BLOCK)).reshape(WA_KV_HEADS, 1, WA_GROUP * BLOCK)
    return band, meta, sink_row


WA_KEYS = 3 * BLOCK + N_META


WA_TILES = 4
WA_SUB = WA_TILES * TOKEN_TILE // BLOCK


def _wa_body(q_ref, kp_ref, kc_ref, kn_ref, vp_ref, vc_ref, vn_ref, km_ref, vm_ref,
             band_ref, mbias_ref, sink_ref, o_ref, s_ref, m_ref, *, npair, steps):
    n_band = 3 * BLOCK
    k_pad = 4 * BLOCK - WA_KEYS
    lo, hi = slice(0, BLOCK), slice(BLOCK, TOKEN_TILE)
    step = pl.program_id(0)
    pair = lax.rem(jnp.minimum(step, steps - 1), npair)
    kinds = ([jnp.where(pair == 0, 0, 1)] + [1] * (WA_SUB - 2)
             + [jnp.where(pair == npair - 1, 2, 1)])

    def key_block(j):
        if j == 0:
            return kp_ref[hi, :]
        if j == WA_SUB + 1:
            return kn_ref[lo, :]
        return kc_ref[(j - 1) * BLOCK:j * BLOCK, :]

    def val_block(j, vrows):
        if j == 0:
            return vp_ref[vrows, hi]
        if j == WA_SUB + 1:
            return vn_ref[vrows, lo]
        return vc_ref[(j - 1) // 2, vrows, (lo, hi)[(j - 1) % 2]]

    @pl.when(step == 0)
    def _():
        s_ref[...] = jnp.zeros_like(s_ref)
        m_ref[...] = jnp.zeros_like(m_ref)

    def stages(s_prev, m_prev, s_next, m_next):
        for sub in range(WA_SUB):
            tile, cols = sub // 2, (lo, hi)[sub % 2]
            for kv in range(WA_KV_HEADS):
                vrows = slice(kv * HEAD_DIM, (kv + 1) * HEAD_DIM)
                m = m_prev[sub, kv, 0:1, :]
                p = jnp.exp2(s_prev[sub, kv, :, :] - m).astype(BF16)
                p = jnp.concatenate([p, jnp.zeros((k_pad, p.shape[1]), BF16)], axis=0)
                v = jnp.concatenate([val_block(sub + g, vrows) for g in range(3)]
                                    + [vm_ref[vrows, :]], axis=1)
                o = jnp.dot(_with_ones_rows(v), p, preferred_element_type=F32)
                l = o[HEAD_DIM:HEAD_DIM + 1] + jnp.exp2(sink_ref[kv] - m)
                o = (o[:HEAD_DIM] / l).astype(BF16)
                for g in range(WA_GROUP):
                    h = kv * WA_GROUP + g
                    o_ref[tile, h * HEAD_DIM:(h + 1) * HEAD_DIM, cols] = (
                        o[:, g * BLOCK:(g + 1) * BLOCK])
                q4 = jnp.concatenate(
                    [q_ref[tile, (kv * WA_GROUP + g) * HEAD_DIM:(kv * WA_GROUP + g + 1) * HEAD_DIM,
                           cols] for g in range(WA_GROUP)], axis=1)
                q4 = _half_padded(q4, kv)
                s_meta = (jnp.dot(km_ref[...], q4, preferred_element_type=F32)
                          + mbias_ref[sub, kv])
                m = jnp.maximum(jnp.max(s_meta, axis=0, keepdims=True), sink_ref[kv])
                for g in range(3):
                    sg = (jnp.dot(key_block(sub + g), q4, preferred_element_type=F32)
                          + band_ref[kinds[sub], kv, g * BLOCK:(g + 1) * BLOCK, :])
                    s_next[sub, kv, g * BLOCK:(g + 1) * BLOCK, :] = sg
                    m = jnp.maximum(m, jnp.max(sg, axis=0, keepdims=True))
                s_next[sub, kv, n_band:WA_KEYS, :] = s_meta
                m_next[sub, kv, 0:1, :] = m

    stages(s_ref, m_ref, s_ref, m_ref)


def _wa_attend(kb, pt, meta_kb, meta_pt, band, meta_bias, sink_row):
    b, n, _ = kb.shape
    nt = n // TOKEN_TILE
    assert pt.shape[1:] == (nt, T_WIDTH, TOKEN_TILE) and nt % WA_TILES == 0
    npair = nt // WA_TILES
    rq, rv = ROW_QB // ATT_WIDTH, ROW_VB // WA_KV_WIDTH
    meta_v = jnp.pad(meta_pt[0, 0, ROW_VB:ROW_VB + WA_KV_WIDTH, :], ((0, 0), (0, BLOCK - N_META)))

    steps = b * npair

    def scored(s):
        return jnp.divmod(jnp.minimum(s, steps - 1), npair)

    def finished(s):
        return jnp.divmod(jnp.maximum(s - 1, 0), npair)

    def before(pair):
        return jnp.maximum(pair * WA_TILES - 1, 0)

    def after(pair):
        return jnp.minimum((pair + 1) * WA_TILES, nt - 1)

    def k_spec(tiles, tile_of):
        def index(s):
            bb, pair = scored(s)
            return bb, tile_of(pair), 0
        return pl.BlockSpec((None, tiles * TOKEN_TILE, WA_KV_WIDTH), index)

    def v_spec(tile_of):
        def index(s):
            bb, pair = finished(s)
            return bb, tile_of(pair), rv, 0
        return pl.BlockSpec((None, None, WA_KV_WIDTH, TOKEN_TILE), index)

    def pair_spec(rows, which, row_block):
        def index(s):
            bb, pair = which(s)
            return bb, pair, row_block, 0
        return pl.BlockSpec((None, WA_TILES, rows, TOKEN_TILE), index)

    lanes = WA_GROUP * BLOCK
    return pl.pallas_call(
        functools.partial(_wa_body, npair=npair, steps=steps),
        grid=(steps + 1,),
        in_specs=[pair_spec(ATT_WIDTH, scored, rq),
                  k_spec(1, before), k_spec(WA_TILES, lambda pair: pair), k_spec(1, after),
                  v_spec(before), pair_spec(WA_KV_WIDTH, finished, rv), v_spec(after),
                  pl.BlockSpec((None, N_META, WA_KV_WIDTH), lambda s: (0, 0, 0)),
                  pl.BlockSpec((WA_KV_WIDTH, BLOCK), lambda s: (0, 0)),
                  pl.BlockSpec(band.shape, lambda s: (0, 0, 0, 0)),
                  pl.BlockSpec((WA_SUB, WA_KV_HEADS, N_META, lanes),
                               lambda s: (scored(s)[1], 0, 0, 0)),
                  pl.BlockSpec((WA_KV_HEADS, 1, lanes), lambda s: (0, 0, 0))],
        out_specs=pair_spec(ATT_WIDTH, finished, 0),
        out_shape=jax.ShapeDtypeStruct((b, nt, ATT_WIDTH, TOKEN_TILE), BF16),
        scratch_shapes=[pltpu.VMEM((WA_SUB, WA_KV_HEADS, WA_KEYS, lanes), F32),
                        pltpu.VMEM((WA_SUB, WA_KV_HEADS, 8, lanes), F32)],
        compiler_params=pltpu.CompilerParams(
            dimension_semantics=("arbitrary",), vmem_limit_bytes=VMEM_LIMIT),
        name="wa_attend",
    )(pt, kb, kb, kb, pt, pt, pt, meta_kb, meta_v, band, meta_bias, sink_row)


def _twice_sigmoid_of_twice(half_x):
    return jnp.tanh(half_x) + 1.0


def _tiles_side_by_side(ref):
    return jnp.concatenate([ref[t] for t in range(ref.shape[0])], axis=1)


def _mix_out_body(x_ref, oa_ref, ob_ref, za_ref, zb_ref, ga_ref, gb_ref,
                  wpa_ref, wpb_ref, wout_ref, fg_ref, y_ref, merged0_ref, merged1_ref):
    step = pl.program_id(0)

    @pl.when(step == 0)
    def _():
        merged1_ref[...] = jnp.zeros_like(merged1_ref)

    def stages(merged_prev, merged_next):
        h = x_ref[...] + lax.dot_general(merged_prev[...], wout_ref[...], _TN,
                                         preferred_element_type=F32)
        ms = jnp.mean(h * h, axis=-1, keepdims=True)
        y_ref[...] = (h * lax.rsqrt(ms + RMS_EPS)) * fg_ref[...]
        za = _tiles_side_by_side(za_ref).astype(F32)
        zb = _tiles_side_by_side(zb_ref).astype(F32)
        ta = (_tiles_side_by_side(oa_ref).astype(F32)
              * (za * _twice_sigmoid_of_twice(za))).astype(BF16)
        tb = (_tiles_side_by_side(ob_ref).astype(F32)
              * (zb * _twice_sigmoid_of_twice(zb))).astype(BF16)
        ya = jnp.dot(wpa_ref[...], ta, preferred_element_type=F32)
        yb = jnp.dot(wpb_ref[...], tb, preferred_element_type=F32)
        merged_next[...] = (
            _twice_sigmoid_of_twice(_tiles_side_by_side(ga_ref).astype(F32)) * ya
            + _twice_sigmoid_of_twice(_tiles_side_by_side(gb_ref).astype(F32)) * yb).astype(BF16)

    @pl.when(step % 2 == 0)
    def _():
        stages(merged1_ref, merged0_ref)

    @pl.when(step % 2 == 1)
    def _():
        stages(merged0_ref, merged1_ref)


def _mix_out(x, oa, ob, pt, wpa_t, wpb_t, wout_half, fg, tm):
    b, n, _ = x.shape
    tile = pt.shape[-1]
    assert n % tm == 0 and tm % tile == 0 and oa.shape == ob.shape == (b, n // tile, ATT_WIDTH, tile)
    steps = b * n // tm
    per_step = tm // tile
    x_tiles = x.reshape(steps, tm, D_MODEL)

    def tiles(a):
        return a.reshape((steps * per_step,) + a.shape[2:])

    def feat(rows, row0):
        return pl.BlockSpec((per_step, rows, tile),
                            lambda s: (jnp.minimum(s, steps - 1), row0 // rows, 0))

    def whole(shape):
        return pl.BlockSpec(shape, lambda s: (0, 0))

    tok = pl.BlockSpec((None, tm, D_MODEL), lambda s: (jnp.maximum(s - 1, 0), 0, 0))
    y = pl.pallas_call(
        _mix_out_body,
        grid=(steps + 1,),
        in_specs=[tok, feat(ATT_WIDTH, 0), feat(ATT_WIDTH, 0),
                  feat(ATT_WIDTH, ROW_ZA), feat(ATT_WIDTH, ROW_ZB),
                  feat(D_MODEL, ROW_GA), feat(D_MODEL, ROW_GB),
                  whole((D_MODEL, ATT_WIDTH)), whole((D_MODEL, ATT_WIDTH)),
                  whole((D_MODEL, D_MODEL)), whole((1, D_MODEL))],
        out_specs=tok,
        out_shape=jax.ShapeDtypeStruct((steps, tm, D_MODEL), F32),
        scratch_shapes=[pltpu.VMEM((D_MODEL, tm), BF16)] * 2,
        compiler_params=pltpu.CompilerParams(
            dimension_semantics=("arbitrary",), vmem_limit_bytes=VMEM_LIMIT),
        name="mix_out",
    )(x_tiles, tiles(oa), tiles(ob), tiles(pt), tiles(pt), tiles(pt), tiles(pt),
      wpa_t, wpb_t, wout_half, fg)
    return y.reshape(b, n, D_MODEL)


def _encode(x, meta, params):
    n = x.shape[1]
    meta_ka, meta_kb, meta_pt = meta
    ka, kb, pt = _norm_proj(x, params["norm_g"], params["w_k"], params["w_t"], PROJ_TM, TOKEN_TILE)
    oa = _na_attend(ka, pt, meta_ka, meta_pt, params["na_bias"])
    band, meta_bias, sink_row = params["wa_bias"]
    assert meta_bias.shape[0] >= n // BLOCK
    ob = _wa_attend(kb, pt, meta_kb, meta_pt, band, meta_bias, sink_row)
    return _mix_out(x, oa, ob, pt, params["w_proj_a_t"], params["w_proj_b_t"], params["w_out_half"],
                    params["final_g"], MIX_TM)


def kernel(x_prompt, x_sample, meta_tokens, norm_g, w_in, na_rpb, sink_logit, w_proj_a, w_proj_b,
           w_out, t5_bias, final_g):
    assert norm_g.shape[0] == 1, "one layer"
    w = w_in[0]
    scale = HEAD_DIM ** -0.5 * LOG2_E
    col = {"qA": (0, 512, scale), "kA": (512, 1024, 1.0), "vA": (1024, 1536, 1.0),
           "zA": (1536, 2048, 0.5), "qB": (2048, 2560, scale), "kB": (2560, 2688, 1.0),
           "vB": (2688, 2816, 1.0), "zB": (2816, 3328, 0.5), "gA": (3328, 4352, 0.5),
           "gB": (4352, 5376, 0.5)}
    w_rows = jnp.transpose(w)
    feature_major = ("gA", "gB", "qA", "vA", "zA", "qB", "zB", "vB")
    w_t = jnp.concatenate([w_rows[col[c][0]:col[c][1]] * col[c][2] for c in feature_major], axis=0)
    max_blocks = max(x_prompt.shape[1], x_sample.shape[1]) // BLOCK
    params = {
        "norm_g": norm_g[0].reshape(1, D_MODEL).astype(F32),
        "w_k": jnp.concatenate([w[:, 512:1024], w[:, 2560:2688]], axis=1).astype(BF16),
        "w_t": w_t.astype(BF16),
        "na_bias": _na_bias_tables(na_rpb[0]),
        "wa_bias": _wa_bias_tables(t5_bias, sink_logit[0], max_blocks),
        "w_proj_a_t": jnp.transpose(w_proj_a[0]).astype(BF16),
        "w_proj_b_t": jnp.transpose(w_proj_b[0]).astype(BF16),
        "w_out_half": (w_out[0] * 0.5).astype(BF16),
        "final_g": final_g.reshape(1, D_MODEL).astype(F32),
    }
    meta = _norm_proj(meta_tokens.astype(F32)[None], params["norm_g"],
                      params["w_k"], params["w_t"], N_META, N_META)
    return (_encode(x_prompt, meta, params), _encode(x_sample, meta, params))
```

```python
import functools
import math

import jax
import jax.numpy as jnp
from jax import lax
from jax.experimental import pallas as pl
from jax.experimental.pallas import tpu as pltpu

F32 = jnp.float32
BF16 = jnp.bfloat16

D_MODEL = 1024
N_META = 16
GRID_W = 64
HEADS = 8
HEAD_DIM = 64
ATT_WIDTH = HEADS * HEAD_DIM
NA_WIN_ROWS = 8
NA_WIN_COLS = 16
WA_KV_HEADS = 2
WA_GROUP = HEADS // WA_KV_HEADS
WA_KV_WIDTH = WA_KV_HEADS * HEAD_DIM
WINDOW = 128
BLOCK = 128
T5_BUCKETS = 32
T5_MAX_DIST = 128
RMS_EPS = 1e-6
NEG_INF = -1e30
LOG2_E = math.log2(math.e)

K_WIDTH = ATT_WIDTH + WA_KV_WIDTH
ROW_GA, ROW_GB, ROW_QA, ROW_VA, ROW_ZA, ROW_QB, ROW_ZB, ROW_VB = (
    0, 1024, 2048, 2560, 3072, 3584, 4096, 4608)
T_WIDTH = 4736

NA_ROWS_PER_STEP = 4
NA_Q = NA_ROWS_PER_STEP * GRID_W
NA_KEY_SEGS = 3
PAIR = 2 * HEAD_DIM
TOKEN_TILE = 256
PROJ_TM = 512
PROJ_ROW_CHUNK = 1184
MIX_TM = 512
VMEM_LIMIT = 52 * 1024 * 1024

_NT = (((1,), (1,)), ((), ()))
_TN = (((0,), (0,)), ((), ()))


def _t5_bucket(rel):
    half = T5_BUCKETS // 2
    exact = half // 2
    ret = jnp.where(rel > 0, half, 0)
    n = jnp.abs(rel)
    nf = jnp.maximum(n, 1).astype(F32)
    large = exact + (jnp.log(nf / exact) / math.log(T5_MAX_DIST / exact)
                     * (half - exact)).astype(jnp.int32)
    large = jnp.minimum(large, half - 1)
    return ret + jnp.where(n < exact, n, large)


def _t5_lookup(tab, rel):
    return jnp.moveaxis(tab[_t5_bucket(rel)], -1, 0)


def _toeplitz(v, rows, cols):
    length = rows + cols - 1
    assert v.shape[-1] == length
    lead = v.shape[:-1]
    flat = jnp.broadcast_to(v[..., None, :], lead + (rows, length)).reshape(lead + (rows * length,))
    flat = flat[..., rows - 1:rows - 1 + rows * (length - 1)]
    return flat.reshape(lead + (rows, length - 1))[..., :cols]


ONES_ROWS = 16


def _with_ones_rows(v):
    return jnp.concatenate([v, jnp.ones((ONES_ROWS, v.shape[1]), v.dtype)], axis=0)


def _half_padded(q, upper):
    zeros = jnp.zeros_like(q)
    return jnp.concatenate([zeros, q] if upper else [q, zeros], axis=0)


def _norm_proj_body(x_ref, g_ref, wk_ref, wt_ref, oka_ref, okb_ref, ot_ref):
    x = x_ref[...]
    ms = jnp.mean(x * x, axis=-1, keepdims=True)
    u = ((x * lax.rsqrt(ms + RMS_EPS)) * g_ref[...]).astype(BF16)
    keys = jnp.dot(u, wk_ref[...], preferred_element_type=F32).astype(BF16)
    oka_ref[...] = keys[:, :ATT_WIDTH]
    okb_ref[...] = keys[:, ATT_WIDTH:]
    n_tiles, _, tile = ot_ref.shape[1:]
    for r in range(0, T_WIDTH, PROJ_ROW_CHUNK):
        rows = lax.dot_general(wt_ref[r:r + PROJ_ROW_CHUNK, :], u, _NT,
                               preferred_element_type=F32).astype(BF16)
        for t in range(n_tiles):
            ot_ref[0, t, r:r + PROJ_ROW_CHUNK, :] = rows[:, t * tile:(t + 1) * tile]


def _norm_proj(x, g, wk, wt, tm, tile):
    b, n, _ = x.shape
    assert n % tm == 0 and tm % tile == 0
    return pl.pallas_call(
        _norm_proj_body,
        grid=(b, n // tm),
        in_specs=[pl.BlockSpec((None, tm, D_MODEL), lambda bb, i: (bb, i, 0)),
                  pl.BlockSpec((1, D_MODEL), lambda bb, i: (0, 0)),
                  pl.BlockSpec((D_MODEL, K_WIDTH), lambda bb, i: (0, 0)),
                  pl.BlockSpec((T_WIDTH, D_MODEL), lambda bb, i: (0, 0))],
        out_specs=[pl.BlockSpec((None, tm, ATT_WIDTH), lambda bb, i: (bb, i, 0)),
                   pl.BlockSpec((None, tm, WA_KV_WIDTH), lambda bb, i: (bb, i, 0)),
                   pl.BlockSpec((1, tm // tile, T_WIDTH, tile), lambda bb, i: (bb, i, 0, 0))],
        out_shape=[jax.ShapeDtypeStruct((b, n, ATT_WIDTH), BF16),
                   jax.ShapeDtypeStruct((b, n, WA_KV_WIDTH), BF16),
                   jax.ShapeDtypeStruct((b, n // tile, T_WIDTH, tile), BF16)],
        compiler_params=pltpu.CompilerParams(
            dimension_semantics=("arbitrary", "arbitrary"), vmem_limit_bytes=VMEM_LIMIT),
        name="norm_proj",
    )(x, g, wk, wt)


def _na_bias_tables(rpb):
    n_t = NA_KEY_SEGS * NA_ROWS_PER_STEP
    lead = GRID_W - NA_WIN_COLS
    v = jnp.pad(rpb.astype(F32) * LOG2_E, ((0, 0), (0, 0), (lead, lead)))
    cols = _toeplitz(v, GRID_W, GRID_W)
    j = jnp.arange(GRID_W)[:, None]
    c = jnp.arange(GRID_W)[None, :]
    cs = jnp.clip(j - NA_WIN_COLS // 2, 0, GRID_W - NA_WIN_COLS)
    cols = jnp.where((c >= cs) & (c < cs + NA_WIN_COLS), cols, NEG_INF)
    cols_t = jnp.swapaxes(cols, -1, -2)
    top_off = NA_WIN_ROWS // 2 - 1
    assert top_off - (NA_ROWS_PER_STEP - 1) >= 0 and top_off + n_t <= cols.shape[1]
    full = jnp.concatenate([cols_t[:, top_off - a:top_off - a + n_t]
                            for a in range(NA_ROWS_PER_STEP)], axis=-1)
    a = jnp.arange(NA_ROWS_PER_STEP)[None, :]
    t = jnp.arange(NA_WIN_T)[:, None]

    def masked(tab, ok):
        ok_lanes = jnp.repeat(ok, GRID_W, axis=1)
        return jnp.where(ok_lanes[None, :, None, :], tab, NEG_INF)

    first8 = (t < NA_WIN_ROWS) & (a >= 0)
    top = jnp.concatenate([full[:, 2 * NA_ROWS_PER_STEP:], full[:, NA_ROWS_PER_STEP:NA_WIN_T]], axis=1)
    tabs = [masked(top, first8),
            masked(full[:, :NA_WIN_T], (t - a >= 0) & (t - a < NA_WIN_ROWS)),
            masked(full[:, :NA_WIN_T], first8)]
    return jnp.stack(tabs).reshape(3, HEADS, NA_WIN_KEYS, NA_Q)


NA_WIN_T = NA_KEY_SEGS * NA_ROWS_PER_STEP - 1
NA_WIN_KEYS = NA_WIN_T * GRID_W
NA_SEG_KEYS = (NA_Q, NA_Q, NA_WIN_KEYS - 2 * NA_Q)
NA_KEYS = NA_WIN_KEYS + N_META


NA_TILES = 2
NA_META_SLOT = NA_SEG_KEYS[-1] - (NA_Q - BLOCK)


def _na_body(q_ref, kb_ref, kc_ref, ka_ref, vb_ref, vc_ref, va_ref, km_ref, vm_ref,
             bias_ref, o_ref, s_ref, m_ref, *, npair, steps):
    n_win = NA_WIN_KEYS
    step = pl.program_id(0)
    pair = lax.rem(jnp.minimum(step, steps - 1), npair)
    kinds = (jnp.where(pair == 0, 0, 1), jnp.where(pair == npair - 1, 2, 1))

    def key_seg(t, g, used, slab):
        if t + g == 0:
            return kb_ref[:used, slab]
        if t + g == NA_TILES + 1:
            return ka_ref[:used, slab]
        return kc_ref[(t + g - 1) * NA_Q:(t + g - 1) * NA_Q + used, slab]

    def val_seg(t, g, rows):
        if t + g == 0:
            return vb_ref[rows, :]
        if t + g == NA_TILES + 1:
            return va_ref[rows, :]
        return vc_ref[t + g - 1, rows, :]

    @pl.when(step == 0)
    def _():
        s_ref[...] = jnp.zeros_like(s_ref)
        m_ref[...] = jnp.zeros_like(m_ref)

    meta_slots = lax.broadcasted_iota(jnp.int32, (HEAD_DIM, BLOCK), 1) >= NA_META_SLOT

    for t in range(NA_TILES):
        for h in range(HEADS):
            rows = slice(h * HEAD_DIM, (h + 1) * HEAD_DIM)
            m_old = m_ref[t, h, 0:1, :]
            slab = slice((h // 2) * PAIR, (h // 2 + 1) * PAIR)
            q = _half_padded(q_ref[t, rows, :], h % 2)
            s_meta = jnp.dot(km_ref[:, slab], q, preferred_element_type=F32)
            m = jnp.max(s_meta, axis=0, keepdims=True)
            o = None
            for g in range(NA_KEY_SEGS):
                used = NA_SEG_KEYS[g]
                seg = slice(g * NA_Q, g * NA_Q + used)
                v = val_seg(t, g, rows)
                if g < NA_KEY_SEGS - 1:
                    p = jnp.exp2(s_ref[t, h, seg, :] - m_old).astype(BF16)
                else:
                    p = jnp.exp2(s_ref[t, h, g * NA_Q:NA_KEYS, :] - m_old).astype(BF16)
                    p = jnp.concatenate(
                        [p, jnp.zeros(((g + 1) * NA_Q - NA_KEYS, NA_Q), BF16)], axis=0)
                    tail = jnp.where(meta_slots, vm_ref[rows, :], v[:, NA_Q - BLOCK:])
                    v = jnp.concatenate([v[:, :NA_Q - BLOCK], tail], axis=1)
                part = jnp.dot(_with_ones_rows(v), p, preferred_element_type=F32)
                o = part if o is None else o + part
                sg = (jnp.dot(key_seg(t, g, used, slab), q, preferred_element_type=F32)
                      + bias_ref[kinds[t], h, seg, :])
                s_ref[t, h, seg, :] = sg
                m = jnp.maximum(m, jnp.max(sg, axis=0, keepdims=True))
            s_ref[t, h, n_win:NA_KEYS, :] = s_meta
            o_ref[t, rows, :] = (o[:HEAD_DIM] / o[HEAD_DIM:HEAD_DIM + 1]).astype(BF16)
            m_ref[t, h, 0:1, :] = m


def _na_attend(ka, pt, meta_ka, meta_pt, bias_tabs):
    b, n, _ = ka.shape
    nblk = n // NA_Q
    assert pt.shape[1:] == (nblk, T_WIDTH, NA_Q) and nblk % NA_TILES == 0
    assert nblk >= 4
    npair = nblk // NA_TILES
    rq, rv = ROW_QA // ATT_WIDTH, ROW_VA // ATT_WIDTH
    meta_v = jnp.pad(meta_pt[0, 0, ROW_VA:ROW_VA + ATT_WIDTH, :],
                     ((0, 0), (NA_META_SLOT, BLOCK - NA_META_SLOT - N_META)))

    steps = b * npair

    def scored(s):
        return jnp.divmod(jnp.minimum(s, steps - 1), npair)

    def finished(s):
        return jnp.divmod(jnp.maximum(s - 1, 0), npair)

    def before(pair):
        return jnp.where(pair == 0, 1, pair * NA_TILES - 1)

    def after(pair):
        return jnp.minimum((pair + 1) * NA_TILES, nblk - 1)

    def k_spec(tiles, tile_of):
        def index(s):
            bb, pair = scored(s)
            return bb, tile_of(pair), 0
        return pl.BlockSpec((None, tiles * NA_Q, ATT_WIDTH), index)

    def v_spec(tile_of):
        def index(s):
            bb, pair = finished(s)
            return bb, tile_of(pair), rv, 0
        return pl.BlockSpec((None, None, ATT_WIDTH, NA_Q), index)

    def pair_spec(which, row_block):
        def index(s):
            bb, pair = which(s)
            return bb, pair, row_block, 0
        return pl.BlockSpec((None, NA_TILES, ATT_WIDTH, NA_Q), index)

    return pl.pallas_call(
        functools.partial(_na_body, npair=npair, steps=steps),
        grid=(steps + 1,),
        in_specs=[pair_spec(scored, rq),
                  k_spec(1, before), k_spec(NA_TILES, lambda pair: pair), k_spec(1, after),
                  v_spec(before), pair_spec(finished, rv), v_spec(after),
                  pl.BlockSpec((None, N_META, ATT_WIDTH), lambda s: (0, 0, 0)),
                  pl.BlockSpec((ATT_WIDTH, BLOCK), lambda s: (0, 0)),
                  pl.BlockSpec(bias_tabs.shape, lambda s: (0, 0, 0, 0))],
        out_specs=pair_spec(finished, 0),
        out_shape=jax.ShapeDtypeStruct((b, nblk, ATT_WIDTH, NA_Q), BF16),
        scratch_shapes=[pltpu.VMEM((NA_TILES, HEADS, NA_KEYS, NA_Q), F32),
                        pltpu.VMEM((NA_TILES, HEADS, 8, NA_Q), F32)],
        compiler_params=pltpu.CompilerParams(
            dimension_semantics=("arbitrary",), vmem_limit_bytes=VMEM_LIMIT),
        name="na_attend",
    )(pt, ka, ka, ka, pt, pt, pt, meta_ka, meta_v, bias_tabs)


def _wa_bias_tables(t5_bias, sink, nb):
    tab = t5_bias.astype(F32) * LOG2_E
    qq = jnp.arange(BLOCK)
    kk = jnp.arange(3 * BLOCK)
    rel = kk[None, :] - BLOCK - qq[:, None]
    dist = jnp.arange(-(2 * BLOCK - 1), 2 * BLOCK)
    band = _toeplitz(_t5_lookup(tab, dist), BLOCK, 3 * BLOCK)
    in_window = jnp.abs(rel) <= WINDOW
    seg = (kk // BLOCK)[None, :]
    kinds = (in_window & (seg != 0), in_window, in_window & (seg != 2))
    band = jnp.stack([jnp.where(ok[None], band, NEG_INF) for ok in kinds])
    band = band.reshape(3, WA_KV_HEADS, WA_GROUP, BLOCK, 3 * BLOCK)
    band = jnp.transpose(band, (0, 1, 4, 2, 3)).reshape(3, WA_KV_HEADS, 3 * BLOCK, WA_GROUP * BLOCK)
    n = nb * BLOCK
    head = T5_MAX_DIST + N_META
    far = tab[T5_BUCKETS // 2 - 1]
    per_dist = jnp.concatenate(
        [_t5_lookup(tab, -jnp.arange(head)),
         jnp.broadcast_to(far[:, None], (HEADS, n + N_META + 1 - head))], axis=1)
    meta = jnp.stack([per_dist[:, N_META - m:N_META - m + n].reshape(HEADS, nb, BLOCK)
                      for m in range(N_META)], axis=2)
    meta = meta.reshape(WA_KV_HEADS, WA_GROUP, nb, N_META, BLOCK)
    meta = jnp.transpose(meta, (2, 0, 3, 1, 4)).reshape(nb, WA_KV_HEADS, N_META, WA_GROUP * BLOCK)
    sink_row = jnp.broadcast_to((sink.astype(F32) * LOG2_E).reshape(WA_KV_HEADS, WA_GROUP, 1),
                                (WA_KV_HEADS, WA_GROUP, BLOCK)).reshape(WA_KV_HEADS, 1, WA_GROUP * BLOCK)
    return band, meta, sink_row


WA_KEYS = 3 * BLOCK + N_META


WA_TILES = 4
WA_SUB = WA_TILES * TOKEN_TILE // BLOCK


def _wa_body(q_ref, kp_ref, kc_ref, kn_ref, vp_ref, vc_ref, vn_ref, km_ref, vm_ref,
             band_ref, mbias_ref, sink_ref, o_ref, s_ref, m_ref, *, npair, steps):
    n_band = 3 * BLOCK
    k_pad = 4 * BLOCK - WA_KEYS
    lo, hi = slice(0, BLOCK), slice(BLOCK, TOKEN_TILE)
    step = pl.program_id(0)
    pair = lax.rem(jnp.minimum(step, steps - 1), npair)
    kinds = ([jnp.where(pair == 0, 0, 1)] + [1] * (WA_SUB - 2)
             + [jnp.where(pair == npair - 1, 2, 1)])

    def key_block(j):
        if j == 0:
            return kp_ref[hi, :]
        if j == WA_SUB + 1:
            return kn_ref[lo, :]
        return kc_ref[(j - 1) * BLOCK:j * BLOCK, :]

    def val_block(j, vrows):
        if j == 0:
            return vp_ref[vrows, hi]
        if j == WA_SUB + 1:
            return vn_ref[vrows, lo]
        return vc_ref[(j - 1) // 2, vrows, (lo, hi)[(j - 1) % 2]]

    @pl.when(step == 0)
    def _():
        s_ref[...] = jnp.zeros_like(s_ref)
        m_ref[...] = jnp.zeros_like(m_ref)

    def stages(s_prev, m_prev, s_next, m_next):
        for sub in range(WA_SUB):
            tile, cols = sub // 2, (lo, hi)[sub % 2]
            for kv in range(WA_KV_HEADS):
                vrows = slice(kv * HEAD_DIM, (kv + 1) * HEAD_DIM)
                m = m_prev[sub, kv, 0:1, :]
                p = jnp.exp2(s_prev[sub, kv, :, :] - m).astype(BF16)
                p = jnp.concatenate([p, jnp.zeros((k_pad, p.shape[1]), BF16)], axis=0)
                v = jnp.concatenate([val_block(sub + g, vrows) for g in range(3)]
                                    + [vm_ref[vrows, :]], axis=1)
                o = jnp.dot(_with_ones_rows(v), p, preferred_element_type=F32)
                l = o[HEAD_DIM:HEAD_DIM + 1] + jnp.exp2(sink_ref[kv] - m)
                o = (o[:HEAD_DIM] / l).astype(BF16)
                for g in range(WA_GROUP):
                    h = kv * WA_GROUP + g
                    o_ref[tile, h * HEAD_DIM:(h + 1) * HEAD_DIM, cols] = (
                        o[:, g * BLOCK:(g + 1) * BLOCK])
                q4 = jnp.concatenate(
                    [q_ref[tile, (kv * WA_GROUP + g) * HEAD_DIM:(kv * WA_GROUP + g + 1) * HEAD_DIM,
                           cols] for g in range(WA_GROUP)], axis=1)
                q4 = _half_padded(q4, kv)
                s_meta = (jnp.dot(km_ref[...], q4, preferred_element_type=F32)
                          + mbias_ref[sub, kv])
                m = jnp.maximum(jnp.max(s_meta, axis=0, keepdims=True), sink_ref[kv])
                for g in range(3):
                    sg = (jnp.dot(key_block(sub + g), q4, preferred_element_type=F32)
                          + band_ref[kinds[sub], kv, g * BLOCK:(g + 1) * BLOCK, :])
                    s_next[sub, kv, g * BLOCK:(g + 1) * BLOCK, :] = sg
                    m = jnp.maximum(m, jnp.max(sg, axis=0, keepdims=True))
                s_next[sub, kv, n_band:WA_KEYS, :] = s_meta
                m_next[sub, kv, 0:1, :] = m

    stages(s_ref, m_ref, s_ref, m_ref)


def _wa_attend(kb, pt, meta_kb, meta_pt, band, meta_bias, sink_row):
    b, n, _ = kb.shape
    nt = n // TOKEN_TILE
    assert pt.shape[1:] == (nt, T_WIDTH, TOKEN_TILE) and nt % WA_TILES == 0
    npair = nt // WA_TILES
    rq, rv = ROW_QB // ATT_WIDTH, ROW_VB // WA_KV_WIDTH
    meta_v = jnp.pad(meta_pt[0, 0, ROW_VB:ROW_VB + WA_KV_WIDTH, :], ((0, 0), (0, BLOCK - N_META)))

    steps = b * npair

    def scored(s):
        return jnp.divmod(jnp.minimum(s, steps - 1), npair)

    def finished(s):
        return jnp.divmod(jnp.maximum(s - 1, 0), npair)

    def before(pair):
        return jnp.maximum(pair * WA_TILES - 1, 0)

    def after(pair):
        return jnp.minimum((pair + 1) * WA_TILES, nt - 1)

    def k_spec(tiles, tile_of):
        def index(s):
            bb, pair = scored(s)
            return bb, tile_of(pair), 0
        return pl.BlockSpec((None, tiles * TOKEN_TILE, WA_KV_WIDTH), index)

    def v_spec(tile_of):
        def index(s):
            bb, pair = finished(s)
            return bb, tile_of(pair), rv, 0
        return pl.BlockSpec((None, None, WA_KV_WIDTH, TOKEN_TILE), index)

    def pair_spec(rows, which, row_block):
        def index(s):
            bb, pair = which(s)
            return bb, pair, row_block, 0
        return pl.BlockSpec((None, WA_TILES, rows, TOKEN_TILE), index)

    lanes = WA_GROUP * BLOCK
    return pl.pallas_call(
        functools.partial(_wa_body, npair=npair, steps=steps),
        grid=(steps + 1,),
        in_specs=[pair_spec(ATT_WIDTH, scored, rq),
                  k_spec(1, before), k_spec(WA_TILES, lambda pair: pair), k_spec(1, after),
                  v_spec(before), pair_spec(WA_KV_WIDTH, finished, rv), v_spec(after),
                  pl.BlockSpec((None, N_META, WA_KV_WIDTH), lambda s: (0, 0, 0)),
                  pl.BlockSpec((WA_KV_WIDTH, BLOCK), lambda s: (0, 0)),
                  pl.BlockSpec(band.shape, lambda s: (0, 0, 0, 0)),
                  pl.BlockSpec((WA_SUB, WA_KV_HEADS, N_META, lanes),
                               lambda s: (scored(s)[1], 0, 0, 0)),
                  pl.BlockSpec((WA_KV_HEADS, 1, lanes), lambda s: (0, 0, 0))],
        out_specs=pair_spec(ATT_WIDTH, finished, 0),
        out_shape=jax.ShapeDtypeStruct((b, nt, ATT_WIDTH, TOKEN_TILE), BF16),
        scratch_shapes=[pltpu.VMEM((WA_SUB, WA_KV_HEADS, WA_KEYS, lanes), F32),
                        pltpu.VMEM((WA_SUB, WA_KV_HEADS, 8, lanes), F32)],
        compiler_params=pltpu.CompilerParams(
            dimension_semantics=("arbitrary",), vmem_limit_bytes=VMEM_LIMIT),
        name="wa_attend",
    )(pt, kb, kb, kb, pt, pt, pt, meta_kb, meta_v, band, meta_bias, sink_row)


def _twice_sigmoid_of_twice(half_x):
    return jnp.tanh(half_x) + 1.0


def _tiles_side_by_side(ref):
    return jnp.concatenate([ref[t] for t in range(ref.shape[0])], axis=1)


def _mix_out_body(x_ref, oa_ref, ob_ref, za_ref, zb_ref, ga_ref, gb_ref,
                  wpa_ref, wpb_ref, wout_ref, fg_ref, y_ref, merged0_ref, merged1_ref):
    step = pl.program_id(0)

    @pl.when(step == 0)
    def _():
        merged1_ref[...] = jnp.zeros_like(merged1_ref)

    def stages(merged_prev, merged_next):
        h = x_ref[...] + lax.dot_general(merged_prev[...], wout_ref[...], _TN,
                                         preferred_element_type=F32)
        ms = jnp.mean(h * h, axis=-1, keepdims=True)
        y_ref[...] = (h * lax.rsqrt(ms + RMS_EPS)) * fg_ref[...]
        za = _tiles_side_by_side(za_ref).astype(F32)
        zb = _tiles_side_by_side(zb_ref).astype(F32)
        ta = (_tiles_side_by_side(oa_ref).astype(F32)
              * (za * _twice_sigmoid_of_twice(za))).astype(BF16)
        tb = (_tiles_side_by_side(ob_ref).astype(F32)
              * (zb * _twice_sigmoid_of_twice(zb))).astype(BF16)
        ya = jnp.dot(wpa_ref[...], ta, preferred_element_type=F32)
        yb = jnp.dot(wpb_ref[...], tb, preferred_element_type=F32)
        merged_next[...] = (
            _twice_sigmoid_of_twice(_tiles_side_by_side(ga_ref).astype(F32)) * ya
            + _twice_sigmoid_of_twice(_tiles_side_by_side(gb_ref).astype(F32)) * yb).astype(BF16)

    @pl.when(step % 2 == 0)
    def _():
        stages(merged1_ref, merged0_ref)

    @pl.when(step % 2 == 1)
    def _():
        stages(merged0_ref, merged1_ref)


def _mix_out(x, oa, ob, pt, wpa_t, wpb_t, wout_half, fg, tm):
    b, n, _ = x.shape
    tile = pt.shape[-1]
    assert n % tm == 0 and tm % tile == 0 and oa.shape == ob.shape == (b, n // tile, ATT_WIDTH, tile)
    steps = b * n // tm
    per_step = tm // tile
    x_tiles = x.reshape(steps, tm, D_MODEL)

    def tiles(a):
        return a.reshape((steps * per_step,) + a.shape[2:])

    def feat(rows, row0):
        return pl.BlockSpec((per_step, rows, tile),
                            lambda s: (jnp.minimum(s, steps - 1), row0 // rows, 0))

    def whole(shape):
        return pl.BlockSpec(shape, lambda s: (0, 0))

    tok = pl.BlockSpec((None, tm, D_MODEL), lambda s: (jnp.maximum(s - 1, 0), 0, 0))
    y = pl.pallas_call(
        _mix_out_body,
        grid=(steps + 1,),
        in_specs=[tok, feat(ATT_WIDTH, 0), feat(ATT_WIDTH, 0),
                  feat(ATT_WIDTH, ROW_ZA), feat(ATT_WIDTH, ROW_ZB),
                  feat(D_MODEL, ROW_GA), feat(D_MODEL, ROW_GB),
                  whole((D_MODEL, ATT_WIDTH)), whole((D_MODEL, ATT_WIDTH)),
                  whole((D_MODEL, D_MODEL)), whole((1, D_MODEL))],
        out_specs=tok,
        out_shape=jax.ShapeDtypeStruct((steps, tm, D_MODEL), F32),
        scratch_shapes=[pltpu.VMEM((D_MODEL, tm), BF16)] * 2,
        compiler_params=pltpu.CompilerParams(
            dimension_semantics=("arbitrary",), vmem_limit_bytes=VMEM_LIMIT),
        name="mix_out",
    )(x_tiles, tiles(oa), tiles(ob), tiles(pt), tiles(pt), tiles(pt), tiles(pt),
      wpa_t, wpb_t, wout_half, fg)
    return y.reshape(b, n, D_MODEL)


def _encode(x, meta, params):
    n = x.shape[1]
    meta_ka, meta_kb, meta_pt = meta
    ka, kb, pt = _norm_proj(x, params["norm_g"], params["w_k"], params["w_t"], PROJ_TM, TOKEN_TILE)
    oa = _na_attend(ka, pt, meta_ka, meta_pt, params["na_bias"])
    band, meta_bias, sink_row = params["wa_bias"]
    assert meta_bias.shape[0] >= n // BLOCK
    ob = _wa_attend(kb, pt, meta_kb, meta_pt, band, meta_bias, sink_row)
    return _mix_out(x, oa, ob, pt, params["w_proj_a_t"], params["w_proj_b_t"], params["w_out_half"],
                    params["final_g"], MIX_TM)


def kernel(x_prompt, x_sample, meta_tokens, norm_g, w_in, na_rpb, sink_logit, w_proj_a, w_proj_b,
           w_out, t5_bias, final_g):
    assert norm_g.shape[0] == 1, "one layer"
    w = w_in[0]
    scale = HEAD_DIM ** -0.5 * LOG2_E
    col = {"qA": (0, 512, scale), "kA": (512, 1024, 1.0), "vA": (1024, 1536, 1.0),
           "zA": (1536, 2048, 0.5), "qB": (2048, 2560, scale), "kB": (2560, 2688, 1.0),
           "vB": (2688, 2816, 1.0), "zB": (2816, 3328, 0.5), "gA": (3328, 4352, 0.5),
           "gB": (4352, 5376, 0.5)}
    w_rows = jnp.transpose(w)
    feature_major = ("gA", "gB", "qA", "vA", "zA", "qB", "zB", "vB")
    w_t = jnp.concatenate([w_rows[col[c][0]:col[c][1]] * col[c][2] for c in feature_major], axis=0)
    max_blocks = max(x_prompt.shape[1], x_sample.shape[1]) // BLOCK
    params = {
        "norm_g": norm_g[0].reshape(1, D_MODEL).astype(F32),
        "w_k": jnp.concatenate([w[:, 512:1024], w[:, 2560:2688]], axis=1).astype(BF16),
        "w_t": w_t.astype(BF16),
        "na_bias": _na_bias_tables(na_rpb[0]),
        "wa_bias": _wa_bias_tables(t5_bias, sink_logit[0], max_blocks),
        "w_proj_a_t": jnp.transpose(w_proj_a[0]).astype(BF16),
        "w_proj_b_t": jnp.transpose(w_proj_b[0]).astype(BF16),
        "w_out_half": (w_out[0] * 0.5).astype(BF16),
        "final_g": final_g.reshape(1, D_MODEL).astype(F32),
    }
    meta = _norm_proj(meta_tokens.astype(F32)[None], params["norm_g"],
                      params["w_k"], params["w_t"], N_META, N_META)
    return (_encode(x_prompt, meta, params), _encode(x_sample, meta, params))
```

```python
import functools
import math

import jax
import jax.numpy as jnp
from jax import lax
from jax.experimental import pallas as pl
from jax.experimental.pallas import tpu as pltpu

F32 = jnp.float32
BF16 = jnp.bfloat16

D_MODEL = 1024
N_META = 16
GRID_W = 64
HEADS = 8
HEAD_DIM = 64
ATT_WIDTH = HEADS * HEAD_DIM
NA_WIN_ROWS = 8
NA_WIN_COLS = 16
WA_KV_HEADS = 2
WA_GROUP = HEADS // WA_KV_HEADS
WA_KV_WIDTH = WA_KV_HEADS * HEAD_DIM
WINDOW = 128
BLOCK = 128
T5_BUCKETS = 32
T5_MAX_DIST = 128
RMS_EPS = 1e-6
NEG_INF = -1e30
LOG2_E = math.log2(math.e)

K_WIDTH = ATT_WIDTH + WA_KV_WIDTH
ROW_GA, ROW_GB, ROW_QA, ROW_VA, ROW_ZA, ROW_QB, ROW_ZB, ROW_VB = (
    0, 1024, 2048, 2560, 3072, 3584, 4096, 4608)
T_WIDTH = 4736

NA_ROWS_PER_STEP = 4
NA_Q = NA_ROWS_PER_STEP * GRID_W
NA_KEY_SEGS = 3
PAIR = 2 * HEAD_DIM
TOKEN_TILE = 256
PROJ_TM = 512
PROJ_ROW_CHUNK = 1184
MIX_TM = 512
VMEM_LIMIT = 52 * 1024 * 1024

_NT = (((1,), (1,)), ((), ()))
_TN = (((0,), (0,)), ((), ()))


def _t5_bucket(rel):
    half = T5_BUCKETS // 2
    exact = half // 2
    ret = jnp.where(rel > 0, half, 0)
    n = jnp.abs(rel)
    nf = jnp.maximum(n, 1).astype(F32)
    large = exact + (jnp.log(nf / exact) / math.log(T5_MAX_DIST / exact)
                     * (half - exact)).astype(jnp.int32)
    large = jnp.minimum(large, half - 1)
    return ret + jnp.where(n < exact, n, large)


def _t5_lookup(tab, rel):
    return jnp.moveaxis(tab[_t5_bucket(rel)], -1, 0)


def _toeplitz(v, rows, cols):
    length = rows + cols - 1
    assert v.shape[-1] == length
    lead = v.shape[:-1]
    flat = jnp.broadcast_to(v[..., None, :], lead + (rows, length)).reshape(lead + (rows * length,))
    flat = flat[..., rows - 1:rows - 1 + rows * (length - 1)]
    return flat.reshape(lead + (rows, length - 1))[..., :cols]


ONES_ROWS = 16


def _with_ones_rows(v):
    return jnp.concatenate([v, jnp.ones((ONES_ROWS, v.shape[1]), v.dtype)], axis=0)


def _half_padded(q, upper):
    zeros = jnp.zeros_like(q)
    return jnp.concatenate([zeros, q] if upper else [q, zeros], axis=0)


def _norm_proj_body(x_ref, g_ref, wk_ref, wt_ref, oka_ref, okb_ref, ot_ref):
    x = x_ref[...]
    ms = jnp.mean(x * x, axis=-1, keepdims=True)
    u = ((x * lax.rsqrt(ms + RMS_EPS)) * g_ref[...]).astype(BF16)
    keys = jnp.dot(u, wk_ref[...], preferred_element_type=F32).astype(BF16)
    oka_ref[...] = keys[:, :ATT_WIDTH]
    okb_ref[...] = keys[:, ATT_WIDTH:]
    n_tiles, _, tile = ot_ref.shape[1:]
    for r in range(0, T_WIDTH, PROJ_ROW_CHUNK):
        rows = lax.dot_general(wt_ref[r:r + PROJ_ROW_CHUNK, :], u, _NT,
                               preferred_element_type=F32).astype(BF16)
        for t in range(n_tiles):
            ot_ref[0, t, r:r + PROJ_ROW_CHUNK, :] = rows[:, t * tile:(t + 1) * tile]


def _norm_proj(x, g, wk, wt, tm, tile):
    b, n, _ = x.shape
    assert n % tm == 0 and tm % tile == 0
    return pl.pallas_call(
        _norm_proj_body,
        grid=(b, n // tm),
        in_specs=[pl.BlockSpec((None, tm, D_MODEL), lambda bb, i: (bb, i, 0)),
                  pl.BlockSpec((1, D_MODEL), lambda bb, i: (0, 0)),
                  pl.BlockSpec((D_MODEL, K_WIDTH), lambda bb, i: (0, 0)),
                  pl.BlockSpec((T_WIDTH, D_MODEL), lambda bb, i: (0, 0))],
        out_specs=[pl.BlockSpec((None, tm, ATT_WIDTH), lambda bb, i: (bb, i, 0)),
                   pl.BlockSpec((None, tm, WA_KV_WIDTH), lambda bb, i: (bb, i, 0)),
                   pl.BlockSpec((1, tm // tile, T_WIDTH, tile), lambda bb, i: (bb, i, 0, 0))],
        out_shape=[jax.ShapeDtypeStruct((b, n, ATT_WIDTH), BF16),
                   jax.ShapeDtypeStruct((b, n, WA_KV_WIDTH), BF16),
                   jax.ShapeDtypeStruct((b, n // tile, T_WIDTH, tile), BF16)],
        compiler_params=pltpu.CompilerParams(
            dimension_semantics=("arbitrary", "arbitrary"), vmem_limit_bytes=VMEM_LIMIT),
        name="norm_proj",
    )(x, g, wk, wt)


def _na_bias_tables(rpb):
    n_t = NA_KEY_SEGS * NA_ROWS_PER_STEP
    lead = GRID_W - NA_WIN_COLS
    v = jnp.pad(rpb.astype(F32) * LOG2_E, ((0, 0), (0, 0), (lead, lead)))
    cols = _toeplitz(v, GRID_W, GRID_W)
    j = jnp.arange(GRID_W)[:, None]
    c = jnp.arange(GRID_W)[None, :]
    cs = jnp.clip(j - NA_WIN_COLS // 2, 0, GRID_W - NA_WIN_COLS)
    cols = jnp.where((c >= cs) & (c < cs + NA_WIN_COLS), cols, NEG_INF)
    cols_t = jnp.swapaxes(cols, -1, -2)
    top_off = NA_WIN_ROWS // 2 - 1
    assert top_off - (NA_ROWS_PER_STEP - 1) >= 0 and top_off + n_t <= cols.shape[1]
    full = jnp.concatenate([cols_t[:, top_off - a:top_off - a + n_t]
                            for a in range(NA_ROWS_PER_STEP)], axis=-1)
    a = jnp.arange(NA_ROWS_PER_STEP)[None, :]
    t = jnp.arange(NA_WIN_T)[:, None]

    def masked(tab, ok):
        ok_lanes = jnp.repeat(ok, GRID_W, axis=1)
        return jnp.where(ok_lanes[None, :, None, :], tab, NEG_INF)

    first8 = (t < NA_WIN_ROWS) & (a >= 0)
    top = jnp.concatenate([full[:, 2 * NA_ROWS_PER_STEP:], full[:, NA_ROWS_PER_STEP:NA_WIN_T]], axis=1)
    tabs = [masked(top, first8),
            masked(full[:, :NA_WIN_T], (t - a >= 0) & (t - a < NA_WIN_ROWS)),
            masked(full[:, :NA_WIN_T], first8)]
    return jnp.stack(tabs).reshape(3, HEADS, NA_WIN_KEYS, NA_Q)


NA_WIN_T = NA_KEY_SEGS * NA_ROWS_PER_STEP - 1
NA_WIN_KEYS = NA_WIN_T * GRID_W
NA_SEG_KEYS = (NA_Q, NA_Q, NA_WIN_KEYS - 2 * NA_Q)
NA_KEYS = NA_WIN_KEYS + N_META


NA_TILES = 2
NA_META_SLOT = NA_SEG_KEYS[-1] - (NA_Q - BLOCK)


def _na_body(q_ref, kb_ref, kc_ref, ka_ref, vb_ref, vc_ref, va_ref, km_ref, vm_ref,
             bias_ref, o_ref, s_ref, m_ref, *, npair, steps):
    n_win = NA_WIN_KEYS
    step = pl.program_id(0)
    pair = lax.rem(jnp.minimum(step, steps - 1), npair)
    kinds = (jnp.where(pair == 0, 0, 1), jnp.where(pair == npair - 1, 2, 1))

    def key_seg(t, g, used, slab):
        if t + g == 0:
            return kb_ref[:used, slab]
        if t + g == NA_TILES + 1:
            return ka_ref[:used, slab]
        return kc_ref[(t + g - 1) * NA_Q:(t + g - 1) * NA_Q + used, slab]

    def val_seg(t, g, rows):
        if t + g == 0:
            return vb_ref[rows, :]
        if t + g == NA_TILES + 1:
            return va_ref[rows, :]
        return vc_ref[t + g - 1, rows, :]

    @pl.when(step == 0)
    def _():
        s_ref[...] = jnp.zeros_like(s_ref)
        m_ref[...] = jnp.zeros_like(m_ref)

    meta_slots = lax.broadcasted_iota(jnp.int32, (HEAD_DIM, BLOCK), 1) >= NA_META_SLOT

    for t in range(NA_TILES):
        for h in range(HEADS):
            rows = slice(h * HEAD_DIM, (h + 1) * HEAD_DIM)
            m_old = m_ref[t, h, 0:1, :]
            slab = slice((h // 2) * PAIR, (h // 2 + 1) * PAIR)
            q = _half_padded(q_ref[t, rows, :], h % 2)
            s_meta = jnp.dot(km_ref[:, slab], q, preferred_element_type=F32)
            m = jnp.max(s_meta, axis=0, keepdims=True)
            o = None
            for g in range(NA_KEY_SEGS):
                used = NA_SEG_KEYS[g]
                seg = slice(g * NA_Q, g * NA_Q + used)
                v = val_seg(t, g, rows)
                if g < NA_KEY_SEGS - 1:
                    p = jnp.exp2(s_ref[t, h, seg, :] - m_old).astype(BF16)
                else:
                    p = jnp.exp2(s_ref[t, h, g * NA_Q:NA_KEYS, :] - m_old).astype(BF16)
                    p = jnp.concatenate(
                        [p, jnp.zeros(((g + 1) * NA_Q - NA_KEYS, NA_Q), BF16)], axis=0)
                    tail = jnp.where(meta_slots, vm_ref[rows, :], v[:, NA_Q - BLOCK:])
                    v = jnp.concatenate([v[:, :NA_Q - BLOCK], tail], axis=1)
                part = jnp.dot(_with_ones_rows(v), p, preferred_element_type=F32)
                o = part if o is None else o + part
                sg = (jnp.dot(key_seg(t, g, used, slab), q, preferred_element_type=F32)
                      + bias_ref[kinds[t], h, seg, :])
                s_ref[t, h, seg, :] = sg
                m = jnp.maximum(m, jnp.max(sg, axis=0, keepdims=True))
            s_ref[t, h, n_win:NA_KEYS, :] = s_meta
            o_ref[t, rows, :] = (o[:HEAD_DIM] / o[HEAD_DIM:HEAD_DIM + 1]).astype(BF16)
            m_ref[t, h, 0:1, :] = m


def _na_attend(ka, pt, meta_ka, meta_pt, bias_tabs):
    b, n, _ = ka.shape
    nblk = n // NA_Q
    assert pt.shape[1:] == (nblk, T_WIDTH, NA_Q) and nblk % NA_TILES == 0
    assert nblk >= 4
    npair = nblk // NA_TILES
    rq, rv = ROW_QA // ATT_WIDTH, ROW_VA // ATT_WIDTH
    meta_v = jnp.pad(meta_pt[0, 0, ROW_VA:ROW_VA + ATT_WIDTH, :],
                     ((0, 0), (NA_META_SLOT, BLOCK - NA_META_SLOT - N_META)))

    steps = b * npair

    def scored(s):
        return jnp.divmod(jnp.minimum(s, steps - 1), npair)

    def finished(s):
        return jnp.divmod(jnp.maximum(s - 1, 0), npair)

    def before(pair):
        return jnp.where(pair == 0, 1, pair * NA_TILES - 1)

    def after(pair):
        return jnp.minimum((pair + 1) * NA_TILES, nblk - 1)

    def k_spec(tiles, tile_of):
        def index(s):
            bb, pair = scored(s)
            return bb, tile_of(pair), 0
        return pl.BlockSpec((None, tiles * NA_Q, ATT_WIDTH), index)

    def v_spec(tile_of):
        def index(s):
            bb, pair = finished(s)
            return bb, tile_of(pair), rv, 0
        return pl.BlockSpec((None, None, ATT_WIDTH, NA_Q), index)

    def pair_spec(which, row_block):
        def index(s):
            bb, pair = which(s)
            return bb, pair, row_block, 0
        return pl.BlockSpec((None, NA_TILES, ATT_WIDTH, NA_Q), index)

    return pl.pallas_call(
        functools.partial(_na_body, npair=npair, steps=steps),
        grid=(steps + 1,),
        in_specs=[pair_spec(scored, rq),
                  k_spec(1, before), k_spec(NA_TILES, lambda pair: pair), k_spec(1, after),
                  v_spec(before), pair_spec(finished, rv), v_spec(after),
                  pl.BlockSpec((None, N_META, ATT_WIDTH), lambda s: (0, 0, 0)),
                  pl.BlockSpec((ATT_WIDTH, BLOCK), lambda s: (0, 0)),
                  pl.BlockSpec(bias_tabs.shape, lambda s: (0, 0, 0, 0))],
        out_specs=pair_spec(finished, 0),
        out_shape=jax.ShapeDtypeStruct((b, nblk, ATT_WIDTH, NA_Q), BF16),
        scratch_shapes=[pltpu.VMEM((NA_TILES, HEADS, NA_KEYS, NA_Q), F32),
                        pltpu.VMEM((NA_TILES, HEADS, 8, NA_Q), F32)],
        compiler_params=pltpu.CompilerParams(
            dimension_semantics=("arbitrary",), vmem_limit_bytes=VMEM_LIMIT),
        name="na_attend",
    )(pt, ka, ka, ka, pt, pt, pt, meta_ka, meta_v, bias_tabs)


def _wa_bias_tables(t5_bias, sink, nb):
    tab = t5_bias.astype(F32) * LOG2_E
    qq = jnp.arange(BLOCK)
    kk = jnp.arange(3 * BLOCK)
    rel = kk[None, :] - BLOCK - qq[:, None]
    dist = jnp.arange(-(2 * BLOCK - 1), 2 * BLOCK)
    band = _toeplitz(_t5_lookup(tab, dist), BLOCK, 3 * BLOCK)
    in_window = jnp.abs(rel) <= WINDOW
    seg = (kk // BLOCK)[None, :]
    kinds = (in_window & (seg != 0), in_window, in_window & (seg != 2))
    band = jnp.stack([jnp.where(ok[None], band, NEG_INF) for ok in kinds])
    band = band.reshape(3, WA_KV_HEADS, WA_GROUP, BLOCK, 3 * BLOCK)
    band = jnp.transpose(band, (0, 1, 4, 2, 3)).reshape(3, WA_KV_HEADS, 3 * BLOCK, WA_GROUP * BLOCK)
    n = nb * BLOCK
    head = T5_MAX_DIST + N_META
    far = tab[T5_BUCKETS // 2 - 1]
    per_dist = jnp.concatenate(
        [_t5_lookup(tab, -jnp.arange(head)),
         jnp.broadcast_to(far[:, None], (HEADS, n + N_META + 1 - head))], axis=1)
    meta = jnp.stack([per_dist[:, N_META - m:N_META - m + n].reshape(HEADS, nb, BLOCK)
                      for m in range(N_META)], axis=2)
    meta = meta.reshape(WA_KV_HEADS, WA_GROUP, nb, N_META, BLOCK)
    meta = jnp.transpose(meta, (2, 0, 3, 1, 4)).reshape(nb, WA_KV_HEADS, N_META, WA_GROUP * BLOCK)
    sink_row = jnp.broadcast_to((sink.astype(F32) * LOG2_E).reshape(WA_KV_HEADS, WA_GROUP, 1),
                                (WA_KV_HEADS, WA_GROUP, BLOCK)).reshape(WA_KV_HEADS, 1, WA_GROUP * BLOCK)
    return band, meta, sink_row


WA_KEYS = 3 * BLOCK + N_META


WA_TILES = 4
WA_SUB = WA_TILES * TOKEN_TILE // BLOCK


def _wa_body(q_ref, kp_ref, kc_ref, kn_ref, vp_ref, vc_ref, vn_ref, km_ref, vm_ref,
             band_ref, mbias_ref, sink_ref, o_ref, s_ref, m_ref, *, npair, steps):
    n_band = 3 * BLOCK
    k_pad = 4 * BLOCK - WA_KEYS
    lo, hi = slice(0, BLOCK), slice(BLOCK, TOKEN_TILE)
    step = pl.program_id(0)
    pair = lax.rem(jnp.minimum(step, steps - 1), npair)
    kinds = ([jnp.where(pair == 0, 0, 1)] + [1] * (WA_SUB - 2)
             + [jnp.where(pair == npair - 1, 2, 1)])

    def key_block(j):
        if j == 0:
            return kp_ref[hi, :]
        if j == WA_SUB + 1:
            return kn_ref[lo, :]
        return kc_ref[(j - 1) * BLOCK:j * BLOCK, :]

    def val_block(j, vrows):
        if j == 0:
            return vp_ref[vrows, hi]
        if j == WA_SUB + 1:
            return vn_ref[vrows, lo]
        return vc_ref[(j - 1) // 2, vrows, (lo, hi)[(j - 1) % 2]]

    @pl.when(step == 0)
    def _():
        s_ref[...] = jnp.zeros_like(s_ref)
        m_ref[...] = jnp.zeros_like(m_ref)

    def stages(s_prev, m_prev, s_next, m_next):
        for sub in range(WA_SUB):
            tile, cols = sub // 2, (lo, hi)[sub % 2]
            for kv in range(WA_KV_HEADS):
                vrows = slice(kv * HEAD_DIM, (kv + 1) * HEAD_DIM)
                m = m_prev[sub, kv, 0:1, :]
                p = jnp.exp2(s_prev[sub, kv, :, :] - m).astype(BF16)
                p = jnp.concatenate([p, jnp.zeros((k_pad, p.shape[1]), BF16)], axis=0)
                v = jnp.concatenate([val_block(sub + g, vrows) for g in range(3)]
                                    + [vm_ref[vrows, :]], axis=1)
                o = jnp.dot(_with_ones_rows(v), p, preferred_element_type=F32)
                l = o[HEAD_DIM:HEAD_DIM + 1] + jnp.exp2(sink_ref[kv] - m)
                o = (o[:HEAD_DIM] / l).astype(BF16)
                for g in range(WA_GROUP):
                    h = kv * WA_GROUP + g
                    o_ref[tile, h * HEAD_DIM:(h + 1) * HEAD_DIM, cols] = (
                        o[:, g * BLOCK:(g + 1) * BLOCK])
                q4 = jnp.concatenate(
                    [q_ref[tile, (kv * WA_GROUP + g) * HEAD_DIM:(kv * WA_GROUP + g + 1) * HEAD_DIM,
                           cols] for g in range(WA_GROUP)], axis=1)
                q4 = _half_padded(q4, kv)
                s_meta = (jnp.dot(km_ref[...], q4, preferred_element_type=F32)
                          + mbias_ref[sub, kv])
                m = jnp.maximum(jnp.max(s_meta, axis=0, keepdims=True), sink_ref[kv])
                for g in range(3):
                    sg = (jnp.dot(key_block(sub + g), q4, preferred_element_type=F32)
                          + band_ref[kinds[sub], kv, g * BLOCK:(g + 1) * BLOCK, :])
                    s_next[sub, kv, g * BLOCK:(g + 1) * BLOCK, :] = sg
                    m = jnp.maximum(m, jnp.max(sg, axis=0, keepdims=True))
                s_next[sub, kv, n_band:WA_KEYS, :] = s_meta
                m_next[sub, kv, 0:1, :] = m

    stages(s_ref, m_ref, s_ref, m_ref)


def _wa_attend(kb, pt, meta_kb, meta_pt, band, meta_bias, sink_row):
    b, n, _ = kb.shape
    nt = n // TOKEN_TILE
    assert pt.shape[1:] == (nt, T_WIDTH, TOKEN_TILE) and nt % WA_TILES == 0
    npair = nt // WA_TILES
    rq, rv = ROW_QB // ATT_WIDTH, ROW_VB // WA_KV_WIDTH
    meta_v = jnp.pad(meta_pt[0, 0, ROW_VB:ROW_VB + WA_KV_WIDTH, :], ((0, 0), (0, BLOCK - N_META)))

    steps = b * npair

    def scored(s):
        return jnp.divmod(jnp.minimum(s, steps - 1), npair)

    def finished(s):
        return jnp.divmod(jnp.maximum(s - 1, 0), npair)

    def before(pair):
        return jnp.maximum(pair * WA_TILES - 1, 0)

    def after(pair):
        return jnp.minimum((pair + 1) * WA_TILES, nt - 1)

    def k_spec(tiles, tile_of):
        def index(s):
            bb, pair = scored(s)
            return bb, tile_of(pair), 0
        return pl.BlockSpec((None, tiles * TOKEN_TILE, WA_KV_WIDTH), index)

    def v_spec(tile_of):
        def index(s):
            bb, pair = finished(s)
            return bb, tile_of(pair), rv, 0
        return pl.BlockSpec((None, None, WA_KV_WIDTH, TOKEN_TILE), index)

    def pair_spec(rows, which, row_block):
        def index(s):
            bb, pair = which(s)
            return bb, pair, row_block, 0
        return pl.BlockSpec((None, WA_TILES, rows, TOKEN_TILE), index)

    lanes = WA_GROUP * BLOCK
    return pl.pallas_call(
        functools.partial(_wa_body, npair=npair, steps=steps),
        grid=(steps + 1,),
        in_specs=[pair_spec(ATT_WIDTH, scored, rq),
                  k_spec(1, before), k_spec(WA_TILES, lambda pair: pair), k_spec(1, after),
                  v_spec(before), pair_spec(WA_KV_WIDTH, finished, rv), v_spec(after),
                  pl.BlockSpec((None, N_META, WA_KV_WIDTH), lambda s: (0, 0, 0)),
                  pl.BlockSpec((WA_KV_WIDTH, BLOCK), lambda s: (0, 0)),
                  pl.BlockSpec(band.shape, lambda s: (0, 0, 0, 0)),
                  pl.BlockSpec((WA_SUB, WA_KV_HEADS, N_META, lanes),
                               lambda s: (scored(s)[1], 0, 0, 0)),
                  pl.BlockSpec((WA_KV_HEADS, 1, lanes), lambda s: (0, 0, 0))],
        out_specs=pair_spec(ATT_WIDTH, finished, 0),
        out_shape=jax.ShapeDtypeStruct((b, nt, ATT_WIDTH, TOKEN_TILE), BF16),
        scratch_shapes=[pltpu.VMEM((WA_SUB, WA_KV_HEADS, WA_KEYS, lanes), F32),
                        pltpu.VMEM((WA_SUB, WA_KV_HEADS, 8, lanes), F32)],
        compiler_params=pltpu.CompilerParams(
            dimension_semantics=("arbitrary",), vmem_limit_bytes=VMEM_LIMIT),
        name="wa_attend",
    )(pt, kb, kb, kb, pt, pt, pt, meta_kb, meta_v, band, meta_bias, sink_row)


def _twice_sigmoid_of_twice(half_x):
    return jnp.tanh(half_x) + 1.0


def _tiles_side_by_side(ref):
    return jnp.concatenate([ref[t] for t in range(ref.shape[0])], axis=1)


def _mix_out_body(x_ref, oa_ref, ob_ref, za_ref, zb_ref, ga_ref, gb_ref,
                  wpa_ref, wpb_ref, wout_ref, fg_ref, y_ref, merged_ref):
    @pl.when(pl.program_id(0) == 0)
    def _():
        merged_ref[...] = jnp.zeros_like(merged_ref)

    h = x_ref[...] + lax.dot_general(merged_ref[...], wout_ref[...], _TN,
                                     preferred_element_type=F32)
    ms = jnp.mean(h * h, axis=-1, keepdims=True)
    y_ref[...] = (h * lax.rsqrt(ms + RMS_EPS)) * fg_ref[...]
    za = _tiles_side_by_side(za_ref).astype(F32)
    zb = _tiles_side_by_side(zb_ref).astype(F32)
    ta = (_tiles_side_by_side(oa_ref).astype(F32)
          * (za * _twice_sigmoid_of_twice(za))).astype(BF16)
    tb = (_tiles_side_by_side(ob_ref).astype(F32)
          * (zb * _twice_sigmoid_of_twice(zb))).astype(BF16)
    ya = jnp.dot(wpa_ref[...], ta, preferred_element_type=F32)
    yb = jnp.dot(wpb_ref[...], tb, preferred_element_type=F32)
    merged_ref[...] = (
        _twice_sigmoid_of_twice(_tiles_side_by_side(ga_ref).astype(F32)) * ya
        + _twice_sigmoid_of_twice(_tiles_side_by_side(gb_ref).astype(F32)) * yb).astype(BF16)


def _mix_out(x, oa, ob, pt, wpa_t, wpb_t, wout_half, fg, tm):
    b, n, _ = x.shape
    tile = pt.shape[-1]
    assert n % tm == 0 and tm % tile == 0 and oa.shape == ob.shape == (b, n // tile, ATT_WIDTH, tile)
    steps = b * n // tm
    per_step = tm // tile
    x_tiles = x.reshape(steps, tm, D_MODEL)

    def tiles(a):
        return a.reshape((steps * per_step,) + a.shape[2:])

    def feat(rows, row0):
        return pl.BlockSpec((per_step, rows, tile),
                            lambda s: (jnp.minimum(s, steps - 1), row0 // rows, 0))

    def whole(shape):
        return pl.BlockSpec(shape, lambda s: (0, 0))

    tok = pl.BlockSpec((None, tm, D_MODEL), lambda s: (jnp.maximum(s - 1, 0), 0, 0))
    y = pl.pallas_call(
        _mix_out_body,
        grid=(steps + 1,),
        in_specs=[tok, feat(ATT_WIDTH, 0), feat(ATT_WIDTH, 0),
                  feat(ATT_WIDTH, ROW_ZA), feat(ATT_WIDTH, ROW_ZB),
                  feat(D_MODEL, ROW_GA), feat(D_MODEL, ROW_GB),
                  whole((D_MODEL, ATT_WIDTH)), whole((D_MODEL, ATT_WIDTH)),
                  whole((D_MODEL, D_MODEL)), whole((1, D_MODEL))],
        out_specs=tok,
        out_shape=jax.ShapeDtypeStruct((steps, tm, D_MODEL), F32),
        scratch_shapes=[pltpu.VMEM((D_MODEL, tm), BF16)],
        compiler_params=pltpu.CompilerParams(
            dimension_semantics=("arbitrary",), vmem_limit_bytes=VMEM_LIMIT),
        name="mix_out",
    )(x_tiles, tiles(oa), tiles(ob), tiles(pt), tiles(pt), tiles(pt), tiles(pt),
      wpa_t, wpb_t, wout_half, fg)
    return y.reshape(b, n, D_MODEL)


def _encode(x, meta, params):
    n = x.shape[1]
    meta_ka, meta_kb, meta_pt = meta
    ka, kb, pt = _norm_proj(x, params["norm_g"], params["w_k"], params["w_t"], PROJ_TM, TOKEN_TILE)
    oa = _na_attend(ka, pt, meta_ka, meta_pt, params["na_bias"])
    band, meta_bias, sink_row = params["wa_bias"]
    assert meta_bias.shape[0] >= n // BLOCK
    ob = _wa_attend(kb, pt, meta_kb, meta_pt, band, meta_bias, sink_row)
    return _mix_out(x, oa, ob, pt, params["w_proj_a_t"], params["w_proj_b_t"], params["w_out_half"],
                    params["final_g"], MIX_TM)


def kernel(x_prompt, x_sample, meta_tokens, norm_g, w_in, na_rpb, sink_logit, w_proj_a, w_proj_b,
           w_out, t5_bias, final_g):
    assert norm_g.shape[0] == 1, "one layer"
    w = w_in[0]
    scale = HEAD_DIM ** -0.5 * LOG2_E
    col = {"qA": (0, 512, scale), "kA": (512, 1024, 1.0), "vA": (1024, 1536, 1.0),
           "zA": (1536, 2048, 0.5), "qB": (2048, 2560, scale), "kB": (2560, 2688, 1.0),
           "vB": (2688, 2816, 1.0), "zB": (2816, 3328, 0.5), "gA": (3328, 4352, 0.5),
           "gB": (4352, 5376, 0.5)}
    feature_major = ("gA", "gB", "qA", "vA", "zA", "qB", "zB", "vB")
    w_t = jnp.transpose(jnp.concatenate(
        [(w[:, col[c][0]:col[c][1]] * col[c][2]).astype(BF16) for c in feature_major], axis=1))
    max_blocks = max(x_prompt.shape[1], x_sample.shape[1]) // BLOCK
    params = {
        "norm_g": norm_g[0].reshape(1, D_MODEL).astype(F32),
        "w_k": jnp.concatenate([w[:, 512:1024], w[:, 2560:2688]], axis=1).astype(BF16),
        "w_t": w_t,
        "na_bias": _na_bias_tables(na_rpb[0]),
        "wa_bias": _wa_bias_tables(t5_bias, sink_logit[0], max_blocks),
        "w_proj_a_t": jnp.transpose(w_proj_a[0]).astype(BF16),
        "w_proj_b_t": jnp.transpose(w_proj_b[0]).astype(BF16),
        "w_out_half": (w_out[0] * 0.5).astype(BF16),
        "final_g": final_g.reshape(1, D_MODEL).astype(F32),
    }
    meta = _norm_proj(meta_tokens.astype(F32)[None], params["norm_g"],
                      params["w_k"], params["w_t"], N_META, N_META)
    return (_encode(x_prompt, meta, params), _encode(x_sample, meta, params))
```

```python
import functools
import math

import jax
import jax.numpy as jnp
from jax import lax
from jax.experimental import pallas as pl
from jax.experimental.pallas import tpu as pltpu

F32 = jnp.float32
BF16 = jnp.bfloat16

D_MODEL = 1024
N_META = 16
GRID_W = 64
HEADS = 8
HEAD_DIM = 64
ATT_WIDTH = HEADS * HEAD_DIM
NA_WIN_ROWS = 8
NA_WIN_COLS = 16
WA_KV_HEADS = 2
WA_GROUP = HEADS // WA_KV_HEADS
WA_KV_WIDTH = WA_KV_HEADS * HEAD_DIM
WINDOW = 128
BLOCK = 128
T5_BUCKETS = 32
T5_MAX_DIST = 128
RMS_EPS = 1e-6
NEG_INF = -1e30
LOG2_E = math.log2(math.e)

K_WIDTH = ATT_WIDTH + WA_KV_WIDTH
ROW_GA, ROW_GB, ROW_QA, ROW_VA, ROW_ZA, ROW_QB, ROW_ZB, ROW_VB = (
    0, 1024, 2048, 2560, 3072, 3584, 4096, 4608)
T_WIDTH = 4736

NA_ROWS_PER_STEP = 4
NA_Q = NA_ROWS_PER_STEP * GRID_W
NA_KEY_SEGS = 3
PAIR = 2 * HEAD_DIM
TOKEN_TILE = 256
PROJ_TM = 512
PROJ_ROW_CHUNK = 1184
MIX_TM = 512
VMEM_LIMIT = 52 * 1024 * 1024

_NT = (((1,), (1,)), ((), ()))
_TN = (((0,), (0,)), ((), ()))


def _t5_bucket(rel):
    half = T5_BUCKETS // 2
    exact = half // 2
    ret = jnp.where(rel > 0, half, 0)
    n = jnp.abs(rel)
    nf = jnp.maximum(n, 1).astype(F32)
    large = exact + (jnp.log(nf / exact) / math.log(T5_MAX_DIST / exact)
                     * (half - exact)).astype(jnp.int32)
    large = jnp.minimum(large, half - 1)
    return ret + jnp.where(n < exact, n, large)


def _t5_lookup(tab, rel):
    return jnp.moveaxis(tab[_t5_bucket(rel)], -1, 0)


def _toeplitz(v, rows, cols):
    length = rows + cols - 1
    assert v.shape[-1] == length
    lead = v.shape[:-1]
    flat = jnp.broadcast_to(v[..., None, :], lead + (rows, length)).reshape(lead + (rows * length,))
    flat = flat[..., rows - 1:rows - 1 + rows * (length - 1)]
    return flat.reshape(lead + (rows, length - 1))[..., :cols]


ONES_ROWS = 16


def _with_ones_rows(v):
    return jnp.concatenate([v, jnp.ones((ONES_ROWS, v.shape[1]), v.dtype)], axis=0)


def _half_padded(q, upper):
    zeros = jnp.zeros_like(q)
    return jnp.concatenate([zeros, q] if upper else [q, zeros], axis=0)


def _norm_proj_body(x_ref, g_ref, wk_ref, wt_ref, oka_ref, okb_ref, ot_ref):
    x = x_ref[...]
    ms = jnp.mean(x * x, axis=-1, keepdims=True)
    u = ((x * lax.rsqrt(ms + RMS_EPS)) * g_ref[...]).astype(BF16)
    keys = jnp.dot(u, wk_ref[...], preferred_element_type=F32).astype(BF16)
    oka_ref[...] = keys[:, :ATT_WIDTH]
    okb_ref[...] = keys[:, ATT_WIDTH:]
    n_tiles, _, tile = ot_ref.shape[1:]
    for r in range(0, T_WIDTH, PROJ_ROW_CHUNK):
        rows = lax.dot_general(wt_ref[r:r + PROJ_ROW_CHUNK, :], u, _NT,
                               preferred_element_type=F32).astype(BF16)
        for t in range(n_tiles):
            ot_ref[0, t, r:r + PROJ_ROW_CHUNK, :] = rows[:, t * tile:(t + 1) * tile]


def _norm_proj(x, g, wk, wt, tm, tile):
    b, n, _ = x.shape
    assert n % tm == 0 and tm % tile == 0
    return pl.pallas_call(
        _norm_proj_body,
        grid=(b, n // tm),
        in_specs=[pl.BlockSpec((None, tm, D_MODEL), lambda bb, i: (bb, i, 0)),
                  pl.BlockSpec((1, D_MODEL), lambda bb, i: (0, 0)),
                  pl.BlockSpec((D_MODEL, K_WIDTH), lambda bb, i: (0, 0)),
                  pl.BlockSpec((T_WIDTH, D_MODEL), lambda bb, i: (0, 0))],
        out_specs=[pl.BlockSpec((None, tm, ATT_WIDTH), lambda bb, i: (bb, i, 0)),
                   pl.BlockSpec((None, tm, WA_KV_WIDTH), lambda bb, i: (bb, i, 0)),
                   pl.BlockSpec((1, tm // tile, T_WIDTH, tile), lambda bb, i: (bb, i, 0, 0))],
        out_shape=[jax.ShapeDtypeStruct((b, n, ATT_WIDTH), BF16),
                   jax.ShapeDtypeStruct((b, n, WA_KV_WIDTH), BF16),
                   jax.ShapeDtypeStruct((b, n // tile, T_WIDTH, tile), BF16)],
        compiler_params=pltpu.CompilerParams(
            dimension_semantics=("arbitrary", "arbitrary"), vmem_limit_bytes=VMEM_LIMIT),
        name="norm_proj",
    )(x, g, wk, wt)


def _na_bias_tables(rpb):
    n_t = NA_KEY_SEGS * NA_ROWS_PER_STEP
    lead = GRID_W - NA_WIN_COLS
    v = jnp.pad(rpb.astype(F32) * LOG2_E, ((0, 0), (0, 0), (lead, lead)))
    cols = _toeplitz(v, GRID_W, GRID_W)
    j = jnp.arange(GRID_W)[:, None]
    c = jnp.arange(GRID_W)[None, :]
    cs = jnp.clip(j - NA_WIN_COLS // 2, 0, GRID_W - NA_WIN_COLS)
    cols = jnp.where((c >= cs) & (c < cs + NA_WIN_COLS), cols, NEG_INF)
    cols_t = jnp.swapaxes(cols, -1, -2)
    top_off = NA_WIN_ROWS // 2 - 1
    assert top_off - (NA_ROWS_PER_STEP - 1) >= 0 and top_off + n_t <= cols.shape[1]
    full = jnp.concatenate([cols_t[:, top_off - a:top_off - a + n_t]
                            for a in range(NA_ROWS_PER_STEP)], axis=-1)
    a = jnp.arange(NA_ROWS_PER_STEP)[None, :]
    t = jnp.arange(NA_WIN_T)[:, None]

    def masked(tab, ok):
        ok_lanes = jnp.repeat(ok, GRID_W, axis=1)
        return jnp.where(ok_lanes[None, :, None, :], tab, NEG_INF)

    first8 = (t < NA_WIN_ROWS) & (a >= 0)
    top = jnp.concatenate([full[:, 2 * NA_ROWS_PER_STEP:], full[:, NA_ROWS_PER_STEP:NA_WIN_T]], axis=1)
    tabs = [masked(top, first8),
            masked(full[:, :NA_WIN_T], (t - a >= 0) & (t - a < NA_WIN_ROWS)),
            masked(full[:, :NA_WIN_T], first8)]
    return jnp.stack(tabs).reshape(3, HEADS, NA_WIN_KEYS, NA_Q)


NA_WIN_T = NA_KEY_SEGS * NA_ROWS_PER_STEP - 1
NA_WIN_KEYS = NA_WIN_T * GRID_W
NA_SEG_KEYS = (NA_Q, NA_Q, NA_WIN_KEYS - 2 * NA_Q)
NA_KEYS = NA_WIN_KEYS + N_META


NA_TILES = 2
NA_META_SLOT = NA_SEG_KEYS[-1] - (NA_Q - BLOCK)


def _na_body(q_ref, kb_ref, kc_ref, ka_ref, vb_ref, vc_ref, va_ref, km_ref, vm_ref,
             bias_ref, o_ref, s_ref, m_ref, *, npair, steps):
    n_win = NA_WIN_KEYS
    step = pl.program_id(0)
    pair = lax.rem(jnp.minimum(step, steps - 1), npair)
    kinds = (jnp.where(pair == 0, 0, 1), jnp.where(pair == npair - 1, 2, 1))

    def key_seg(t, g, used, slab):
        if t + g == 0:
            return kb_ref[:used, slab]
        if t + g == NA_TILES + 1:
            return ka_ref[:used, slab]
        return kc_ref[(t + g - 1) * NA_Q:(t + g - 1) * NA_Q + used, slab]

    def val_seg(t, g, rows):
        if t + g == 0:
            return vb_ref[rows, :]
        if t + g == NA_TILES + 1:
            return va_ref[rows, :]
        return vc_ref[t + g - 1, rows, :]

    @pl.when(step == 0)
    def _():
        s_ref[...] = jnp.zeros_like(s_ref)
        m_ref[...] = jnp.zeros_like(m_ref)

    meta_slots = lax.broadcasted_iota(jnp.int32, (HEAD_DIM, BLOCK), 1) >= NA_META_SLOT

    for t in range(NA_TILES):
        for h in range(HEADS):
            rows = slice(h * HEAD_DIM, (h + 1) * HEAD_DIM)
            m_old = m_ref[t, h, 0:1, :]
            slab = slice((h // 2) * PAIR, (h // 2 + 1) * PAIR)
            q = _half_padded(q_ref[t, rows, :], h % 2)
            s_meta = jnp.dot(km_ref[:, slab], q, preferred_element_type=F32)
            m = jnp.max(s_meta, axis=0, keepdims=True)
            o = None
            for g in range(NA_KEY_SEGS):
                used = NA_SEG_KEYS[g]
                seg = slice(g * NA_Q, g * NA_Q + used)
                v = val_seg(t, g, rows)
                if g < NA_KEY_SEGS - 1:
                    p = jnp.exp2(s_ref[t, h, seg, :] - m_old).astype(BF16)
                else:
                    p = jnp.exp2(s_ref[t, h, g * NA_Q:NA_KEYS, :] - m_old).astype(BF16)
                    p = jnp.concatenate(
                        [p, jnp.zeros(((g + 1) * NA_Q - NA_KEYS, NA_Q), BF16)], axis=0)
                    tail = jnp.where(meta_slots, vm_ref[rows, :], v[:, NA_Q - BLOCK:])
                    v = jnp.concatenate([v[:, :NA_Q - BLOCK], tail], axis=1)
                part = jnp.dot(_with_ones_rows(v), p, preferred_element_type=F32)
                o = part if o is None else o + part
                sg = (jnp.dot(key_seg(t, g, used, slab), q, preferred_element_type=F32)
                      + bias_ref[kinds[t], h, seg, :])
                s_ref[t, h, seg, :] = sg
                m = jnp.maximum(m, jnp.max(sg, axis=0, keepdims=True))
            s_ref[t, h, n_win:NA_KEYS, :] = s_meta
            o_ref[t, rows, :] = (o[:HEAD_DIM] / o[HEAD_DIM:HEAD_DIM + 1]).astype(BF16)
            m_ref[t, h, 0:1, :] = m


def _na_attend(ka, pt, meta_ka, meta_pt, bias_tabs):
    b, n, _ = ka.shape
    nblk = n // NA_Q
    assert pt.shape[1:] == (nblk, T_WIDTH, NA_Q) and nblk % NA_TILES == 0
    assert nblk >= 4
    npair = nblk // NA_TILES
    rq, rv = ROW_QA // ATT_WIDTH, ROW_VA // ATT_WIDTH
    meta_v = jnp.pad(meta_pt[0, 0, ROW_VA:ROW_VA + ATT_WIDTH, :],
                     ((0, 0), (NA_META_SLOT, BLOCK - NA_META_SLOT - N_META)))

    steps = b * npair

    def scored(s):
        return jnp.divmod(jnp.minimum(s, steps - 1), npair)

    def finished(s):
        return jnp.divmod(jnp.maximum(s - 1, 0), npair)

    def before(pair):
        return jnp.where(pair == 0, 1, pair * NA_TILES - 1)

    def after(pair):
        return jnp.minimum((pair + 1) * NA_TILES, nblk - 1)

    def k_spec(tiles, tile_of):
        def index(s):
            bb, pair = scored(s)
            return bb, tile_of(pair), 0
        return pl.BlockSpec((None, tiles * NA_Q, ATT_WIDTH), index)

    def v_spec(tile_of):
        def index(s):
            bb, pair = finished(s)
            return bb, tile_of(pair), rv, 0
        return pl.BlockSpec((None, None, ATT_WIDTH, NA_Q), index)

    def pair_spec(which, row_block):
        def index(s):
            bb, pair = which(s)
            return bb, pair, row_block, 0
        return pl.BlockSpec((None, NA_TILES, ATT_WIDTH, NA_Q), index)

    return pl.pallas_call(
        functools.partial(_na_body, npair=npair, steps=steps),
        grid=(steps + 1,),
        in_specs=[pair_spec(scored, rq),
                  k_spec(1, before), k_spec(NA_TILES, lambda pair: pair), k_spec(1, after),
                  v_spec(before), pair_spec(finished, rv), v_spec(after),
                  pl.BlockSpec((None, N_META, ATT_WIDTH), lambda s: (0, 0, 0)),
                  pl.BlockSpec((ATT_WIDTH, BLOCK), lambda s: (0, 0)),
                  pl.BlockSpec(bias_tabs.shape, lambda s: (0, 0, 0, 0))],
        out_specs=pair_spec(finished, 0),
        out_shape=jax.ShapeDtypeStruct((b, nblk, ATT_WIDTH, NA_Q), BF16),
        scratch_shapes=[pltpu.VMEM((NA_TILES, HEADS, NA_KEYS, NA_Q), F32),
                        pltpu.VMEM((NA_TILES, HEADS, 8, NA_Q), F32)],
        compiler_params=pltpu.CompilerParams(
            dimension_semantics=("arbitrary",), vmem_limit_bytes=VMEM_LIMIT),
        name="na_attend",
    )(pt, ka, ka, ka, pt, pt, pt, meta_ka, meta_v, bias_tabs)


def _wa_bias_tables(t5_bias, sink, nb):
    tab = t5_bias.astype(F32) * LOG2_E
    qq = jnp.arange(BLOCK)
    kk = jnp.arange(3 * BLOCK)
    rel = kk[None, :] - BLOCK - qq[:, None]
    dist = jnp.arange(-(2 * BLOCK - 1), 2 * BLOCK)
    band = _toeplitz(_t5_lookup(tab, dist), BLOCK, 3 * BLOCK)
    in_window = jnp.abs(rel) <= WINDOW
    seg = (kk // BLOCK)[None, :]
    kinds = (in_window & (seg != 0), in_window, in_window & (seg != 2))
    band = jnp.stack([jnp.where(ok[None], band, NEG_INF) for ok in kinds])
    band = band.reshape(3, WA_KV_HEADS, WA_GROUP, BLOCK, 3 * BLOCK)
    band = jnp.transpose(band, (0, 1, 4, 2, 3)).reshape(3, WA_KV_HEADS, 3 * BLOCK, WA_GROUP * BLOCK)
    n = nb * BLOCK
    head = T5_MAX_DIST + N_META
    far = tab[T5_BUCKETS // 2 - 1]
    per_dist = jnp.concatenate(
        [_t5_lookup(tab, -jnp.arange(head)),
         jnp.broadcast_to(far[:, None], (HEADS, n + N_META + 1 - head))], axis=1)
    meta = jnp.stack([per_dist[:, N_META - m:N_META - m + n].reshape(HEADS, nb, BLOCK)
                      for m in range(N_META)], axis=2)
    meta = meta.reshape(WA_KV_HEADS, WA_GROUP, nb, N_META, BLOCK)
    meta = jnp.transpose(meta, (2, 0, 3, 1, 4)).reshape(nb, WA_KV_HEADS, N_META, WA_GROUP * BLOCK)
    sink_row = jnp.broadcast_to((sink.astype(F32) * LOG2_E).reshape(WA_KV_HEADS, WA_GROUP, 1),
                                (WA_KV_HEADS, WA_GROUP, BLOCK)).reshape(WA_KV_HEADS, 1, WA_GROUP * BLOCK)
    return band, meta, sink_row


WA_KEYS = 3 * BLOCK + N_META


WA_TILES = 8
WA_SUB = WA_TILES * TOKEN_TILE // BLOCK


def _wa_body(q_ref, kp_ref, kc_ref, kn_ref, vp_ref, vc_ref, vn_ref, km_ref, vm_ref,
             band_ref, mbias_ref, sink_ref, o_ref, s_ref, m_ref, *, npair, steps):
    n_band = 3 * BLOCK
    k_pad = 4 * BLOCK - WA_KEYS
    lo, hi = slice(0, BLOCK), slice(BLOCK, TOKEN_TILE)
    step = pl.program_id(0)
    pair = lax.rem(jnp.minimum(step, steps - 1), npair)
    kinds = ([jnp.where(pair == 0, 0, 1)] + [1] * (WA_SUB - 2)
             + [jnp.where(pair == npair - 1, 2, 1)])

    def key_block(j):
        if j == 0:
            return kp_ref[hi, :]
        if j == WA_SUB + 1:
            return kn_ref[lo, :]
        return kc_ref[(j - 1) * BLOCK:j * BLOCK, :]

    def val_block(j, vrows):
        if j == 0:
            return vp_ref[vrows, hi]
        if j == WA_SUB + 1:
            return vn_ref[vrows, lo]
        return vc_ref[(j - 1) // 2, vrows, (lo, hi)[(j - 1) % 2]]

    @pl.when(step == 0)
    def _():
        s_ref[...] = jnp.zeros_like(s_ref)
        m_ref[...] = jnp.zeros_like(m_ref)

    def stages(s_prev, m_prev, s_next, m_next):
        for sub in range(WA_SUB):
            tile, cols = sub // 2, (lo, hi)[sub % 2]
            for kv in range(WA_KV_HEADS):
                vrows = slice(kv * HEAD_DIM, (kv + 1) * HEAD_DIM)
                m = m_prev[sub, kv, 0:1, :]
                p = jnp.exp2(s_prev[sub, kv, :, :] - m).astype(BF16)
                p = jnp.concatenate([p, jnp.zeros((k_pad, p.shape[1]), BF16)], axis=0)
                v = jnp.concatenate([val_block(sub + g, vrows) for g in range(3)]
                                    + [vm_ref[vrows, :]], axis=1)
                o = jnp.dot(_with_ones_rows(v), p, preferred_element_type=F32)
                l = o[HEAD_DIM:HEAD_DIM + 1] + jnp.exp2(sink_ref[kv] - m)
                o = (o[:HEAD_DIM] / l).astype(BF16)
                for g in range(WA_GROUP):
                    h = kv * WA_GROUP + g
                    o_ref[tile, h * HEAD_DIM:(h + 1) * HEAD_DIM, cols] = (
                        o[:, g * BLOCK:(g + 1) * BLOCK])
                q4 = jnp.concatenate(
                    [q_ref[tile, (kv * WA_GROUP + g) * HEAD_DIM:(kv * WA_GROUP + g + 1) * HEAD_DIM,
                           cols] for g in range(WA_GROUP)], axis=1)
                q4 = _half_padded(q4, kv)
                s_meta = (jnp.dot(km_ref[...], q4, preferred_element_type=F32)
                          + mbias_ref[sub, kv])
                m = jnp.maximum(jnp.max(s_meta, axis=0, keepdims=True), sink_ref[kv])
                for g in range(3):
                    sg = (jnp.dot(key_block(sub + g), q4, preferred_element_type=F32)
                          + band_ref[kinds[sub], kv, g * BLOCK:(g + 1) * BLOCK, :])
                    s_next[sub, kv, g * BLOCK:(g + 1) * BLOCK, :] = sg
                    m = jnp.maximum(m, jnp.max(sg, axis=0, keepdims=True))
                s_next[sub, kv, n_band:WA_KEYS, :] = s_meta
                m_next[sub, kv, 0:1, :] = m

    stages(s_ref, m_ref, s_ref, m_ref)


def _wa_attend(kb, pt, meta_kb, meta_pt, band, meta_bias, sink_row):
    b, n, _ = kb.shape
    nt = n // TOKEN_TILE
    assert pt.shape[1:] == (nt, T_WIDTH, TOKEN_TILE) and nt % WA_TILES == 0
    npair = nt // WA_TILES
    rq, rv = ROW_QB // ATT_WIDTH, ROW_VB // WA_KV_WIDTH
    meta_v = jnp.pad(meta_pt[0, 0, ROW_VB:ROW_VB + WA_KV_WIDTH, :], ((0, 0), (0, BLOCK - N_META)))

    steps = b * npair

    def scored(s):
        return jnp.divmod(jnp.minimum(s, steps - 1), npair)

    def finished(s):
        return jnp.divmod(jnp.maximum(s - 1, 0), npair)

    def before(pair):
        return jnp.maximum(pair * WA_TILES - 1, 0)

    def after(pair):
        return jnp.minimum((pair + 1) * WA_TILES, nt - 1)

    def k_spec(tiles, tile_of):
        def index(s):
            bb, pair = scored(s)
            return bb, tile_of(pair), 0
        return pl.BlockSpec((None, tiles * TOKEN_TILE, WA_KV_WIDTH), index)

    def v_spec(tile_of):
        def index(s):
            bb, pair = finished(s)
            return bb, tile_of(pair), rv, 0
        return pl.BlockSpec((None, None, WA_KV_WIDTH, TOKEN_TILE), index)

    def pair_spec(rows, which, row_block):
        def index(s):
            bb, pair = which(s)
            return bb, pair, row_block, 0
        return pl.BlockSpec((None, WA_TILES, rows, TOKEN_TILE), index)

    lanes = WA_GROUP * BLOCK
    return pl.pallas_call(
        functools.partial(_wa_body, npair=npair, steps=steps),
        grid=(steps + 1,),
        in_specs=[pair_spec(ATT_WIDTH, scored, rq),
                  k_spec(1, before), k_spec(WA_TILES, lambda pair: pair), k_spec(1, after),
                  v_spec(before), pair_spec(WA_KV_WIDTH, finished, rv), v_spec(after),
                  pl.BlockSpec((None, N_META, WA_KV_WIDTH), lambda s: (0, 0, 0)),
                  pl.BlockSpec((WA_KV_WIDTH, BLOCK), lambda s: (0, 0)),
                  pl.BlockSpec(band.shape, lambda s: (0, 0, 0, 0)),
                  pl.BlockSpec((WA_SUB, WA_KV_HEADS, N_META, lanes),
                               lambda s: (scored(s)[1], 0, 0, 0)),
                  pl.BlockSpec((WA_KV_HEADS, 1, lanes), lambda s: (0, 0, 0))],
        out_specs=pair_spec(ATT_WIDTH, finished, 0),
        out_shape=jax.ShapeDtypeStruct((b, nt, ATT_WIDTH, TOKEN_TILE), BF16),
        scratch_shapes=[pltpu.VMEM((WA_SUB, WA_KV_HEADS, WA_KEYS, lanes), F32),
                        pltpu.VMEM((WA_SUB, WA_KV_HEADS, 8, lanes), F32)],
        compiler_params=pltpu.CompilerParams(
            dimension_semantics=("arbitrary",), vmem_limit_bytes=VMEM_LIMIT),
        name="wa_attend",
    )(pt, kb, kb, kb, pt, pt, pt, meta_kb, meta_v, band, meta_bias, sink_row)


def _twice_sigmoid_of_twice(half_x):
    return jnp.tanh(half_x) + 1.0


def _tiles_side_by_side(ref):
    return jnp.concatenate([ref[t] for t in range(ref.shape[0])], axis=1)


def _mix_out_body(x_ref, oa_ref, ob_ref, za_ref, zb_ref, ga_ref, gb_ref,
                  wpa_ref, wpb_ref, wout_ref, fg_ref, y_ref, merged_ref):
    @pl.when(pl.program_id(0) == 0)
    def _():
        merged_ref[...] = jnp.zeros_like(merged_ref)

    h = x_ref[...] + lax.dot_general(merged_ref[...], wout_ref[...], _TN,
                                     preferred_element_type=F32)
    ms = jnp.mean(h * h, axis=-1, keepdims=True)
    y_ref[...] = (h * lax.rsqrt(ms + RMS_EPS)) * fg_ref[...]
    za = _tiles_side_by_side(za_ref).astype(F32)
    zb = _tiles_side_by_side(zb_ref).astype(F32)
    ta = (_tiles_side_by_side(oa_ref).astype(F32)
          * (za * _twice_sigmoid_of_twice(za))).astype(BF16)
    tb = (_tiles_side_by_side(ob_ref).astype(F32)
          * (zb * _twice_sigmoid_of_twice(zb))).astype(BF16)
    ya = jnp.dot(wpa_ref[...], ta, preferred_element_type=F32)
    yb = jnp.dot(wpb_ref[...], tb, preferred_element_type=F32)
    merged_ref[...] = (
        _twice_sigmoid_of_twice(_tiles_side_by_side(ga_ref).astype(F32)) * ya
        + _twice_sigmoid_of_twice(_tiles_side_by_side(gb_ref).astype(F32)) * yb).astype(BF16)


def _mix_out(x, oa, ob, pt, wpa_t, wpb_t, wout_half, fg, tm):
    b, n, _ = x.shape
    tile = pt.shape[-1]
    assert n % tm == 0 and tm % tile == 0 and oa.shape == ob.shape == (b, n // tile, ATT_WIDTH, tile)
    steps = b * n // tm
    per_step = tm // tile
    x_tiles = x.reshape(steps, tm, D_MODEL)

    def tiles(a):
        return a.reshape((steps * per_step,) + a.shape[2:])

    def feat(rows, row0):
        return pl.BlockSpec((per_step, rows, tile),
                            lambda s: (jnp.minimum(s, steps - 1), row0 // rows, 0))

    def whole(shape):
        return pl.BlockSpec(shape, lambda s: (0, 0))

    tok = pl.BlockSpec((None, tm, D_MODEL), lambda s: (jnp.maximum(s - 1, 0), 0, 0))
    y = pl.pallas_call(
        _mix_out_body,
        grid=(steps + 1,),
        in_specs=[tok, feat(ATT_WIDTH, 0), feat(ATT_WIDTH, 0),
                  feat(ATT_WIDTH, ROW_ZA), feat(ATT_WIDTH, ROW_ZB),
                  feat(D_MODEL, ROW_GA), feat(D_MODEL, ROW_GB),
                  whole((D_MODEL, ATT_WIDTH)), whole((D_MODEL, ATT_WIDTH)),
                  whole((D_MODEL, D_MODEL)), whole((1, D_MODEL))],
        out_specs=tok,
        out_shape=jax.ShapeDtypeStruct((steps, tm, D_MODEL), F32),
        scratch_shapes=[pltpu.VMEM((D_MODEL, tm), BF16)],
        compiler_params=pltpu.CompilerParams(
            dimension_semantics=("arbitrary",), vmem_limit_bytes=VMEM_LIMIT),
        name="mix_out",
    )(x_tiles, tiles(oa), tiles(ob), tiles(pt), tiles(pt), tiles(pt), tiles(pt),
      wpa_t, wpb_t, wout_half, fg)
    return y.reshape(b, n, D_MODEL)


def _encode(x, meta, params):
    n = x.shape[1]
    meta_ka, meta_kb, meta_pt = meta
    ka, kb, pt = _norm_proj(x, params["norm_g"], params["w_k"], params["w_t"], PROJ_TM, TOKEN_TILE)
    oa = _na_attend(ka, pt, meta_ka, meta_pt, params["na_bias"])
    band, meta_bias, sink_row = params["wa_bias"]
    assert meta_bias.shape[0] >= n // BLOCK
    ob = _wa_attend(kb, pt, meta_kb, meta_pt, band, meta_bias, sink_row)
    return _mix_out(x, oa, ob, pt, params["w_proj_a_t"], params["w_proj_b_t"], params["w_out_half"],
                    params["final_g"], MIX_TM)


def kernel(x_prompt, x_sample, meta_tokens, norm_g, w_in, na_rpb, sink_logit, w_proj_a, w_proj_b,
           w_out, t5_bias, final_g):
    assert norm_g.shape[0] == 1, "one layer"
    w = w_in[0]
    scale = HEAD_DIM ** -0.5 * LOG2_E
    col = {"qA": (0, 512, scale), "kA": (512, 1024, 1.0), "vA": (1024, 1536, 1.0),
           "zA": (1536, 2048, 0.5), "qB": (2048, 2560, scale), "kB": (2560, 2688, 1.0),
           "vB": (2688, 2816, 1.0), "zB": (2816, 3328, 0.5), "gA": (3328, 4352, 0.5),
           "gB": (4352, 5376, 0.5)}
    feature_major = ("gA", "gB", "qA", "vA", "zA", "qB", "zB", "vB")
    w_t = jnp.transpose(jnp.concatenate(
        [(w[:, col[c][0]:col[c][1]] * col[c][2]).astype(BF16) for c in feature_major], axis=1))
    max_blocks = max(x_prompt.shape[1], x_sample.shape[1]) // BLOCK
    params = {
        "norm_g": norm_g[0].reshape(1, D_MODEL).astype(F32),
        "w_k": jnp.concatenate([w[:, 512:1024], w[:, 2560:2688]], axis=1).astype(BF16),
        "w_t": w_t,
        "na_bias": _na_bias_tables(na_rpb[0]),
        "wa_bias": _wa_bias_tables(t5_bias, sink_logit[0], max_blocks),
        "w_proj_a_t": jnp.transpose(w_proj_a[0]).astype(BF16),
        "w_proj_b_t": jnp.transpose(w_proj_b[0]).astype(BF16),
        "w_out_half": (w_out[0] * 0.5).astype(BF16),
        "final_g": final_g.reshape(1, D_MODEL).astype(F32),
    }
    meta = _norm_proj(meta_tokens.astype(F32)[None], params["norm_g"],
                      params["w_k"], params["w_t"], N_META, N_META)
    return (_encode(x_prompt, meta, params), _encode(x_sample, meta, params))
```
